```python
import jax, jax.numpy as jnp
from jax import lax
import numpy as np

D_MODEL = 1024
BATCH = 8
SEQ = 4096
DEPTH = 2
DEC_BATCH = 16
DEC_SEQ = 32
PAST_LEN = 2048

CHUNK = 64
N_HEADS = 8
N_KV_HEADS = 2
HEAD_DIM = 64
ATTN_DIM = N_HEADS * HEAD_DIM
KV_DIM = N_KV_HEADS * HEAD_DIM
IDX_HEADS = 4
IDX_DIM = 64
TOPK_MAX = 256
Q_BLOCK = 128
GMLP_CHUNK = 128
GMLP_GROUPS = 4
GMLP_DIM = 512
GMLP_GROUP_DIM = GMLP_DIM // GMLP_GROUPS
D_FF = 4 * D_MODEL
ROPE_THETA = 500000.0
ROPE_DIM = HEAD_DIM // 4
IDX_ROPE_DIM = IDX_DIM // 4
ALPHA = (2 * DEPTH) ** 0.25
BETA = (8 * DEPTH) ** -0.25
LN_EPS = 1e-5
IN_DIM = ATTN_DIM + 2 * KV_DIM + IDX_HEADS * IDX_DIM + IDX_DIM + IDX_HEADS + 2 * GMLP_DIM + 2 * D_MODEL

kernel_name = 'dsa_gmlp_gated_hybrid_stream_step'


def _layer_norm(x, g, b):
    xf = x.astype(jnp.float32)
    mu = jnp.mean(xf, axis=-1, keepdims=True)
    var = jnp.mean(jnp.square(xf - mu), axis=-1, keepdims=True)
    y = (xf - mu) * lax.rsqrt(var + LN_EPS)
    return (y * g.astype(jnp.float32) + b.astype(jnp.float32)).astype(x.dtype)


def _rope(x, pos, rot_dim):
    half = rot_dim // 2
    freqs = ROPE_THETA ** (-jnp.arange(half, dtype=jnp.float32) * 2.0 / rot_dim)
    ang = pos.astype(jnp.float32)[:, None] * freqs[None, :]
    ang = ang.reshape(ang.shape[:1] + (1,) * (x.ndim - 3) + (half,))
    cos, sin = jnp.cos(ang), jnp.sin(ang)
    xf = x.astype(jnp.float32)
    x1, x2, rest = xf[..., :half], xf[..., half:rot_dim], xf[..., rot_dim:]
    out = jnp.concatenate([x1 * cos - x2 * sin, x1 * sin + x2 * cos, rest], axis=-1)
    return out.astype(x.dtype)


def _mixer_inputs(x, pos, w_in, idx_k_g, idx_k_b):
    B, T, _ = x.shape
    sizes = (ATTN_DIM, KV_DIM, KV_DIM, IDX_HEADS * IDX_DIM, IDX_DIM, IDX_HEADS,
             GMLP_DIM, GMLP_DIM, D_MODEL, D_MODEL)
    points = [int(p) for p in np.cumsum(sizes)[:-1]]
    q, k, v, qi, ki, wi, u, vg, ga, gb = jnp.split(x @ w_in, points, axis=-1)
    q = _rope(q.reshape(B, T, N_HEADS, HEAD_DIM), pos, ROPE_DIM)
    k = _rope(k.reshape(B, T, N_KV_HEADS, HEAD_DIM), pos, ROPE_DIM)
    v = v.reshape(B, T, N_KV_HEADS, HEAD_DIM)
    qi = _rope(qi.reshape(B, T, IDX_HEADS, IDX_DIM), pos, IDX_ROPE_DIM)
    ki = _rope(_layer_norm(ki, idx_k_g, idx_k_b), pos, IDX_ROPE_DIM)
    return q, k, v, qi, ki, wi, u, vg, ga, gb


def _sparse_attend(q, qi, wi, qpos, k, v, ki, topk):
    B, Q = q.shape[:2]
    S = k.shape[1]
    limit = (qpos // CHUNK + 1) * CHUNK
    dots = jnp.einsum('bqhd,bsd->bqhs', qi.astype(jnp.float32), ki.astype(jnp.float32)) * IDX_DIM ** -0.5
    score = jnp.einsum('bqh,bqhs->bqs', wi.astype(jnp.float32) * IDX_HEADS ** -0.5, jax.nn.relu(dots))
    admissible = jnp.arange(S)[None, :] < limit[:, None]
    score = jnp.where(admissible[None], score, -jnp.inf)
    _, sel = lax.top_k(score, topk)
    valid = sel < limit[None, :, None]
    k_sel = jax.vmap(lambda kk, ii: kk[ii])(k, sel)
    v_sel = jax.vmap(lambda vv, ii: vv[ii])(v, sel)
    qg = q.reshape(B, Q, N_KV_HEADS, N_HEADS // N_KV_HEADS, HEAD_DIM)
    logits = jnp.einsum('bqngd,bqknd->bqngk', qg.astype(jnp.float32), k_sel.astype(jnp.float32)) * HEAD_DIM ** -0.5
    logits = jnp.where(valid[:, :, None, None, :], logits, -jnp.inf)
    p = jax.nn.softmax(logits, axis=-1)
    o = jnp.einsum('bqngk,bqknd->bqngd', p.astype(v_sel.dtype), v_sel)
    return o.reshape(B, Q, ATTN_DIM)


def _prompt_attention(q, k, v, qi, ki, wi, topk):
    B, T = q.shape[:2]
    nb = T // Q_BLOCK

    def blockify(a):
        return jnp.moveaxis(a.reshape((B, nb, Q_BLOCK) + a.shape[2:]), 1, 0)

    def one_block(args):
        qb, qib, wib, blk = args
        qpos = blk * Q_BLOCK + jnp.arange(Q_BLOCK)
        return _sparse_attend(qb, qib, wib, qpos, k, v, ki, topk)

    out = lax.map(one_block, (blockify(q), blockify(qi), blockify(wi), jnp.arange(nb)))
    return jnp.moveaxis(out, 0, 1).reshape(B, T, ATTN_DIM)


def _causal_ws(w_s):
    mask = jnp.tril(jnp.ones((GMLP_CHUNK, GMLP_CHUNK), dtype=bool))
    return jnp.where(mask[None], w_s, jnp.zeros_like(w_s))


def _sgu_prompt(u, vn, w_s, b_s):
    B, T, _ = u.shape
    vr = vn.reshape(B, T // GMLP_CHUNK, GMLP_CHUNK, GMLP_GROUPS, GMLP_GROUP_DIM)
    mix = jnp.einsum('gts,bcsgd->bctgd', _causal_ws(w_s), vr) + b_s.T[:, :, None]
    return u * mix.reshape(B, T, GMLP_DIM)


def _sgu_sample(u, vn, w_s, b_s):
    B, T, _ = u.shape
    vr = vn.reshape(B, T, GMLP_GROUPS, GMLP_GROUP_DIM)
    mix = jnp.einsum('gts,bsgd->btgd', _causal_ws(w_s)[:, :T, :T], vr) + b_s[:, :T].T[:, :, None]
    return u * mix.reshape(B, T, GMLP_DIM)


def _post_block(x, o_a, o_b, ga, gb, w_pa, w_pb, w_out, ln1_g, ln1_b, w_ff1, w_ff2, ln2_g, ln2_b):
    merged = jax.nn.sigmoid(ga) * (o_a @ w_pa) + jax.nn.sigmoid(gb) * (o_b @ w_pb)
    x = _layer_norm(ALPHA * x + merged @ w_out, ln1_g, ln1_b)
    ff = jnp.square(jax.nn.relu(x @ w_ff1)) @ w_ff2
    return _layer_norm(ALPHA * x + ff, ln2_g, ln2_b)


def setup_inputs(seed: int = 0) -> dict:
    key = jax.random.key(seed)
    ks = jax.random.split(key, 24)

    def nrm(k, shape, scale):
        return jax.random.normal(k, shape, jnp.float32) * scale

    return {
        'x_prompt': nrm(ks[0], (BATCH, SEQ, D_MODEL), 1.0),
        'x_sample': nrm(ks[1], (DEC_BATCH, DEC_SEQ, D_MODEL), 1.0),
        'cache_k': nrm(ks[2], (DEPTH, DEC_BATCH, PAST_LEN, N_KV_HEADS, HEAD_DIM), 1.0),
        'cache_v': nrm(ks[3], (DEPTH, DEC_BATCH, PAST_LEN, N_KV_HEADS, HEAD_DIM), 1.0),
        'cache_idx_k': nrm(ks[4], (DEPTH, DEC_BATCH, PAST_LEN, IDX_DIM), 1.0),
        'w_in': nrm(ks[5], (DEPTH, D_MODEL, IN_DIM), D_MODEL ** -0.5),
        'idx_k_g': 1.0 + nrm(ks[6], (DEPTH, IDX_DIM), 0.02),
        'idx_k_b': nrm(ks[7], (DEPTH, IDX_DIM), 0.02),
        'sgu_ln_g': 1.0 + nrm(ks[8], (DEPTH, GMLP_DIM), 0.02),
        'sgu_ln_b': nrm(ks[9], (DEPTH, GMLP_DIM), 0.02),
        'w_s': nrm(ks[10], (DEPTH, GMLP_GROUPS, GMLP_CHUNK, GMLP_CHUNK), 0.5 * GMLP_CHUNK ** -0.5),
        'b_s': 1.0 + nrm(ks[11], (DEPTH, GMLP_GROUPS, GMLP_CHUNK), 0.01),
        'w_pa': nrm(ks[12], (DEPTH, ATTN_DIM, D_MODEL), ATTN_DIM ** -0.5),
        'w_pb': nrm(ks[13], (DEPTH, GMLP_DIM, D_MODEL), GMLP_DIM ** -0.5),
        'w_out': nrm(ks[14], (DEPTH, D_MODEL, D_MODEL), BETA * D_MODEL ** -0.5),
        'ln1_g': 1.0 + nrm(ks[15], (DEPTH, D_MODEL), 0.02),
        'ln1_b': nrm(ks[16], (DEPTH, D_MODEL), 0.02),
        'w_ff1': nrm(ks[17], (DEPTH, D_MODEL, D_FF), D_MODEL ** -0.5),
        'w_ff2': nrm(ks[18], (DEPTH, D_FF, D_MODEL), BETA * D_FF ** -0.5),
        'ln2_g': 1.0 + nrm(ks[19], (DEPTH, D_MODEL), 0.02),
        'ln2_b': nrm(ks[20], (DEPTH, D_MODEL), 0.02),
    }


def reference(x_prompt, x_sample, cache_k, cache_v, cache_idx_k, w_in, idx_k_g, idx_k_b,
              sgu_ln_g, sgu_ln_b, w_s, b_s, w_pa, w_pb, w_out, ln1_g, ln1_b,
              w_ff1, w_ff2, ln2_g, ln2_b):
    t_prompt = x_prompt.shape[1]
    t_sample = x_sample.shape[1]
    past = cache_k.shape[2]
    pos_p = jnp.arange(t_prompt)
    pos_s = past + jnp.arange(t_sample)
    topk_p = min(TOPK_MAX, t_prompt // 4)
    topk_s = min(TOPK_MAX, (past + t_sample) // 4)

    xp, xs = x_prompt, x_sample
    pk, pv, pik, sk, sv, sik, ssv = [], [], [], [], [], [], []
    for l in range(DEPTH):
        q, k, v, qi, ki, wi, u, vg, ga, gb = _mixer_inputs(xp, pos_p, w_in[l], idx_k_g[l], idx_k_b[l])
        o_a = _prompt_attention(q, k, v, qi, ki, wi, topk_p)
        o_b = _sgu_prompt(u, _layer_norm(vg, sgu_ln_g[l], sgu_ln_b[l]), w_s[l], b_s[l])
        xp = _post_block(xp, o_a, o_b, ga, gb, w_pa[l], w_pb[l], w_out[l], ln1_g[l], ln1_b[l],
                         w_ff1[l], w_ff2[l], ln2_g[l], ln2_b[l])
        pk.append(k)
        pv.append(v)
        pik.append(ki)

        q, k, v, qi, ki, wi, u, vg, ga, gb = _mixer_inputs(xs, pos_s, w_in[l], idx_k_g[l], idx_k_b[l])
        k_all = jnp.concatenate([cache_k[l], k], axis=1)
        v_all = jnp.concatenate([cache_v[l], v], axis=1)
        ki_all = jnp.concatenate([cache_idx_k[l], ki], axis=1)
        o_a = _sparse_attend(q, qi, wi, pos_s, k_all, v_all, ki_all, topk_s)
        vn = _layer_norm(vg, sgu_ln_g[l], sgu_ln_b[l])
        o_b = _sgu_sample(u, vn, w_s[l], b_s[l])
        xs = _post_block(xs, o_a, o_b, ga, gb, w_pa[l], w_pb[l], w_out[l], ln1_g[l], ln1_b[l],
                         w_ff1[l], w_ff2[l], ln2_g[l], ln2_b[l])
        sk.append(k)
        sv.append(v)
        sik.append(ki)
        ssv.append(vn)

    return (xp, xs, jnp.stack(pk), jnp.stack(pv), jnp.stack(pik),
            jnp.stack(sk), jnp.stack(sv), jnp.stack(sik), jnp.stack(ssv))
```

```python
import functools

import jax
import jax.numpy as jnp
from jax import lax
from jax.experimental import pallas as pl
from jax.experimental.pallas import tpu as pltpu

F32 = jnp.float32
BF16 = jnp.bfloat16

D_MODEL = 1024
N_HEADS = 8
N_KV_HEADS = 2
HEAD_DIM = 64
ATTN_DIM = N_HEADS * HEAD_DIM
KV_DIM = N_KV_HEADS * HEAD_DIM
IDX_HEADS = 4
IDX_DIM = 64
TOPK_MAX = 256
CHUNK = 64
GMLP_CHUNK = 128
GMLP_GROUPS = 4
GMLP_DIM = 512
GMLP_GROUP_DIM = GMLP_DIM // GMLP_GROUPS
D_FF = 4 * D_MODEL
ROPE_THETA = 500000.0
ROPE_DIM = HEAD_DIM // 4
DEPTH = 2
ALPHA = (2 * DEPTH) ** 0.25
LN_EPS = 1e-5

LANES = 128
Q_SCALE = HEAD_DIM ** -0.5
IDX_SCALE = IDX_DIM ** -0.5
WI_SCALE = IDX_HEADS ** -0.5
NEG_INF = float("-inf")
M_INIT = -1e30

OFF_Q, OFF_K, OFF_V, OFF_QI, OFF_KIW, OFF_U, OFF_VG = 0, 512, 640, 768, 1024, 1152, 1664
IN_COLS = 2176
KEY_TILE = 512
VMEM_LIMIT = 56 * 1024 * 1024


def _cparams(sem):
    return pltpu.CompilerParams(dimension_semantics=sem, vmem_limit_bytes=VMEM_LIMIT)


def _layer_norm(x, g, b):
    mu = jnp.mean(x, axis=-1, keepdims=True)
    d = x - mu
    var = jnp.mean(d * d, axis=-1, keepdims=True)
    return d * lax.rsqrt(var + LN_EPS) * g + b


def _rope(y, cos, sn, sp):
    w = y.shape[-1]
    half = ROPE_DIM // 2
    return y * cos + pltpu.roll(y, w - half, 1) * sn + pltpu.roll(y, half, 1) * sp


def _tile_lanes(a, n):
    return a if n == 1 else jnp.concatenate([a] * n, axis=1)


def _inproj_body(x_ref, w_ref, cos_ref, sn_ref, sp_ref, kig_ref, kib_ref, sg_ref, sb_ref,
                 ws_ref, bst_ref,
                 qs_ref, kf_ref, vf_ref, kb_ref, vb_ref, qia_ref, kiw_ref, kihl_ref, ob_ref,
                 *vn_refs, cl):
    tm = x_ref.shape[0]
    xb = x_ref[...].astype(BF16)

    def proj(lo, n):
        return jnp.dot(xb, w_ref[:, lo:lo + n], preferred_element_type=F32)

    cos, sn, sp = cos_ref[...], sn_ref[...], sp_ref[...]
    lane = lax.broadcasted_iota(jnp.int32, (tm, LANES), 1)
    left = lane < IDX_DIM

    q = _rope(proj(OFF_Q, ATTN_DIM), _tile_lanes(cos, 4), _tile_lanes(sn, 4), _tile_lanes(sp, 4))
    qs_ref[...] = (q * Q_SCALE).astype(BF16)

    k = _rope(proj(OFF_K, KV_DIM), cos, sn, sp)
    kf_ref[...] = k
    kb_ref[...] = k.astype(BF16)
    v = proj(OFF_V, KV_DIM)
    vf_ref[...] = v
    vb_ref[...] = v.astype(BF16)

    qi = _rope(proj(OFF_QI, IDX_HEADS * IDX_DIM), _tile_lanes(cos, 2), _tile_lanes(sn, 2),
               _tile_lanes(sp, 2)) * IDX_SCALE
    qi_hi = qi.astype(BF16).astype(F32)
    qi_lo = qi - qi_hi
    for t in range(IDX_HEADS // 2):
        h_t = qi_hi[:, t * LANES:(t + 1) * LANES]
        l_t = qi_lo[:, t * LANES:(t + 1) * LANES]
        a0 = jnp.where(left, h_t, pltpu.roll(l_t, IDX_DIM, 1))
        a1 = jnp.where(left, pltpu.roll(h_t, IDX_DIM, 1), l_t)
        for j, a in enumerate((a0, a1)):
            base = (2 * t + j) * 2 * LANES
            qia_ref[:, base:base + 2 * LANES] = jnp.concatenate([a, a], axis=1).astype(BF16)

    slab = proj(OFF_KIW, LANES)
    mu = jnp.sum(jnp.where(left, slab, 0.0), axis=1, keepdims=True) * (1.0 / IDX_DIM)
    d = jnp.where(left, slab - mu, 0.0)
    var = jnp.sum(d * d, axis=1, keepdims=True) * (1.0 / IDX_DIM)
    kin = d * lax.rsqrt(var + LN_EPS) * kig_ref[...] + kib_ref[...]
    slab = jnp.where(left, kin, slab)
    slab = _rope(slab, jnp.where(left, cos, 1.0), jnp.where(left, sn, 0.0), jnp.where(left, sp, 0.0))
    kiw_ref[...] = slab
    kz = jnp.where(left, slab, 0.0)
    k_hi = kz.astype(BF16).astype(F32)
    k_lo = kz - k_hi
    hh = k_hi + pltpu.roll(k_hi, IDX_DIM, 1)
    ll = k_lo + pltpu.roll(k_lo, IDX_DIM, 1)
    kihl_ref[...] = jnp.concatenate([hh, ll], axis=1).astype(BF16)

    u = proj(OFF_U, GMLP_DIM)
    vn = _layer_norm(proj(OFF_VG, GMLP_DIM), sg_ref[...], sb_ref[...])
    if vn_refs:
        vn_refs[0][...] = vn
    vnb = vn.astype(BF16)
    r_i = lax.broadcasted_iota(jnp.int32, (cl, cl), 0)
    c_i = lax.broadcasted_iota(jnp.int32, (cl, cl), 1)
    for g in range(GMLP_GROUPS):
        wg = jnp.where(r_i >= c_i, ws_ref[g], 0.0).astype(BF16)
        bg = bst_ref[:, g:g + 1]
        gs = slice(g * GMLP_GROUP_DIM, (g + 1) * GMLP_GROUP_DIM)
        for c in range(tm // cl):
            rs = slice(c * cl, (c + 1) * cl)
            mix = jnp.dot(wg, vnb[rs, gs], preferred_element_type=F32) + bg
            ob_ref[rs, gs] = (u[rs, gs] * mix).astype(BF16)


def _inproj(x2d, w_a, tabs, kig, kib, sg, sb, ws, bst, *, tm, cl, emit_vn):
    m = x2d.shape[0]
    cos, sn, sp = tabs
    nt = cos.shape[0] // tm
    row = lambda w: pl.BlockSpec((tm, w), lambda i: (i, 0))
    tab = pl.BlockSpec((tm, LANES), lambda i: (i % nt, 0))
    full = lambda a: pl.BlockSpec(a.shape, lambda i: (0,) * a.ndim)
    out_shapes = [
        jax.ShapeDtypeStruct((m, ATTN_DIM), BF16),
        jax.ShapeDtypeStruct((m, KV_DIM), F32),
        jax.ShapeDtypeStruct((m, KV_DIM), F32),
        jax.ShapeDtypeStruct((m, KV_DIM), BF16),
        jax.ShapeDtypeStruct((m, KV_DIM), BF16),
        jax.ShapeDtypeStruct((m, IDX_HEADS * 4 * IDX_DIM), BF16),
        jax.ShapeDtypeStruct((m, LANES), F32),
        jax.ShapeDtypeStruct((m, 4 * IDX_DIM), BF16),
        jax.ShapeDtypeStruct((m, GMLP_DIM), BF16),
    ]
    out_specs = [row(ATTN_DIM), row(KV_DIM), row(KV_DIM), row(KV_DIM), row(KV_DIM),
                 row(IDX_HEADS * 4 * IDX_DIM), row(LANES), row(4 * IDX_DIM), row(GMLP_DIM)]
    if emit_vn:
        out_shapes.append(jax.ShapeDtypeStruct((m, GMLP_DIM), F32))
        out_specs.append(row(GMLP_DIM))
    return pl.pallas_call(
        functools.partial(_inproj_body, cl=cl),
        grid=(m // tm,),
        in_specs=[row(D_MODEL), full(w_a), tab, tab, tab, full(kig), full(kib), full(sg), full(sb),
                  full(ws), full(bst)],
        out_specs=out_specs,
        out_shape=out_shapes,
        compiler_params=_cparams(("parallel",)),
        name="inproj",
    )(x2d, w_a, cos, sn, sp, kig, kib, sg, sb, ws, bst)


def _attn_body(qs_ref, qia_ref, kiw_ref, kihl_ref, kb_ref, vb_ref, o_ref,
               keys_ref, bias_ref, m_ref, l_ref, acc_ref, *, qb, pos0, s_true, topk):
    i = pl.program_id(1)
    cpt = KEY_TILE // LANES
    row = lax.broadcasted_iota(jnp.int32, (qb, 1), 0)
    qpos = pos0 + i * qb + row
    limit = jnp.minimum((qpos // CHUNK + 1) * CHUNK, s_true)
    kv_len = jnp.minimum(((pos0 + i * qb + qb - 1) // CHUNK + 1) * CHUNK, s_true)
    n_kt = (kv_len + KEY_TILE - 1) // KEY_TILE
    n_c = (kv_len + LANES - 1) // LANES
    lane = lax.broadcasted_iota(jnp.int32, (qb, LANES), 1)

    wi = kiw_ref[:, IDX_DIM:IDX_DIM + IDX_HEADS] * WI_SCALE

    def score_tile(t, carry):
        kt = kihl_ref[pl.ds(pl.multiple_of(t * KEY_TILE, KEY_TILE), KEY_TILE), :]
        sc = jnp.zeros((qb, KEY_TILE), F32)
        for h in range(IDX_HEADS):
            a = qia_ref[:, h * 4 * IDX_DIM:(h + 1) * 4 * IDX_DIM]
            dots = lax.dot_general(a, kt, (((1,), (1,)), ((), ())), preferred_element_type=F32)
            sc = sc + wi[:, h:h + 1] * jnp.maximum(dots, 0.0)
        for j in range(cpt):
            col = t * KEY_TILE + j * LANES + lane
            s_j = jnp.where(col < limit, sc[:, j * LANES:(j + 1) * LANES], NEG_INF)
            bits = lax.bitcast_convert_type(s_j, jnp.int32)
            keys_ref[t * cpt + j] = bits ^ ((bits >> 31) & 0x7FFFFFFF)
        return carry

    lax.fori_loop(0, n_kt, score_tile, 0)

    def count(pred, thr):
        def body(c, acc):
            return acc + jnp.where(pred(keys_ref[c], thr), 1.0, 0.0)
        acc = lax.fori_loop(0, n_c, body, jnp.zeros((qb, LANES), F32))
        return jnp.sum(acc, axis=1, keepdims=True)

    def bit_step(it, thr):
        cand = thr + jnp.left_shift(jnp.int32(1), 31 - it)
        cnt = count(lambda kk, t_: kk >= t_, cand)
        return jnp.where(cnt >= topk, cand, thr)

    thr = lax.fori_loop(0, 32, bit_step, jnp.full((qb, LANES), -2 ** 31, jnp.int32))
    need = topk - count(lambda kk, t_: kk > t_, thr)

    r_i = lax.broadcasted_iota(jnp.int32, (LANES, LANES), 0)
    c_i = lax.broadcasted_iota(jnp.int32, (LANES, LANES), 1)
    upper = jnp.where(r_i <= c_i, 1.0, 0.0).astype(BF16)

    def tie_chunk(c, off):
        kk = keys_ref[c]
        eq = kk == thr
        local = jnp.dot(jnp.where(eq, 1.0, 0.0).astype(BF16), upper, preferred_element_type=F32)
        take = eq & (local + off <= need)
        sel = ((kk > thr) | take) & (c * LANES + lane < limit)
        bias_ref[c] = jnp.where(sel, 0.0, NEG_INF)
        return off + local[:, LANES - 1:LANES]

    lax.fori_loop(0, n_kt * cpt, tie_chunk, jnp.zeros((qb, 1), F32))

    grp = N_HEADS // N_KV_HEADS
    for n in range(N_KV_HEADS):
        qn = jnp.concatenate(
            [qs_ref[:, (grp * n + g) * HEAD_DIM:(grp * n + g + 1) * HEAD_DIM] for g in range(grp)], axis=0)
        m_ref[...] = jnp.full(m_ref.shape, M_INIT, F32)
        l_ref[...] = jnp.zeros(l_ref.shape, F32)
        acc_ref[...] = jnp.zeros(acc_ref.shape, F32)

        def kv_tile(t, carry):
            r0 = pl.multiple_of(t * KEY_TILE, KEY_TILE)
            kt = kb_ref[pl.ds(r0, KEY_TILE), n * HEAD_DIM:(n + 1) * HEAD_DIM]
            vt = vb_ref[pl.ds(r0, KEY_TILE), n * HEAD_DIM:(n + 1) * HEAD_DIM]
            s = lax.dot_general(qn, kt, (((1,), (1,)), ((), ())), preferred_element_type=F32)
            bias = jnp.concatenate([bias_ref[t * cpt + j] for j in range(cpt)], axis=1)
            s = (s.reshape(grp, qb, KEY_TILE) + bias[None]).reshape(grp * qb, KEY_TILE)
            m_prev = m_ref[...]
            m_new = jnp.maximum(m_prev, jnp.max(s, axis=1, keepdims=True))
            p = jnp.exp(s - m_new[:, :1])
            corr = jnp.exp(m_prev - m_new)
            l_ref[...] = corr * l_ref[...] + jnp.sum(p, axis=1, keepdims=True)
            acc_ref[...] = acc_ref[...] * corr[:, :HEAD_DIM] + jnp.dot(
                p.astype(BF16), vt, preferred_element_type=F32)
            m_ref[...] = m_new
            return carry

        lax.fori_loop(0, n_kt, kv_tile, 0)
        o_n = acc_ref[...] / l_ref[...][:, :HEAD_DIM]
        for g in range(grp):
            h = grp * n + g
            o_ref[:, h * HEAD_DIM:(h + 1) * HEAD_DIM] = o_n[g * qb:(g + 1) * qb].astype(o_ref.dtype)


def _attention(qs, qia, kiw, kihl, kb, vb, *, qb, pos0, s_true, topk):
    b, t, _ = qs.shape
    s_pad = kihl.shape[1]
    assert t % qb == 0 and s_pad % KEY_TILE == 0
    grp = N_HEADS // N_KV_HEADS
    qblk = lambda w: pl.BlockSpec((None, qb, w), lambda bi, i: (bi, i, 0))
    kblk = lambda w: pl.BlockSpec((None, s_pad, w), lambda bi, i: (bi, 0, 0))
    return pl.pallas_call(
        functools.partial(_attn_body, qb=qb, pos0=pos0, s_true=s_true, topk=topk),
        grid=(b, t // qb),
        in_specs=[qblk(ATTN_DIM), qblk(IDX_HEADS * 4 * IDX_DIM), qblk(LANES),
                  kblk(4 * IDX_DIM), kblk(KV_DIM), kblk(KV_DIM)],
        out_specs=qblk(ATTN_DIM),
        out_shape=jax.ShapeDtypeStruct((b, t, ATTN_DIM), BF16),
        scratch_shapes=[
            pltpu.VMEM((s_pad // LANES, qb, LANES), jnp.int32),
            pltpu.VMEM((s_pad // LANES, qb, LANES), F32),
            pltpu.VMEM((grp * qb, LANES), F32),
            pltpu.VMEM((grp * qb, LANES), F32),
            pltpu.VMEM((grp * qb, HEAD_DIM), F32),
        ],
        compiler_params=_cparams(("parallel", "arbitrary")),
        name="sparse_attn",
    )(qs, qia, kiw, kihl, kb, vb)


def _merge_body(x_ref, oa_ref, ob_ref, wga_ref, wgb_ref, wpa_ref, wpb_ref, wout_ref, g_ref, b_ref, o_ref):
    x = x_ref[...]
    xb = x.astype(BF16)
    ga = jnp.dot(xb, wga_ref[...], preferred_element_type=F32)
    gb = jnp.dot(xb, wgb_ref[...], preferred_element_type=F32)
    pa = jnp.dot(oa_ref[...], wpa_ref[...], preferred_element_type=F32)
    pb = jnp.dot(ob_ref[...], wpb_ref[...], preferred_element_type=F32)
    merged = jax.nn.sigmoid(ga) * pa + jax.nn.sigmoid(gb) * pb
    h = ALPHA * x + jnp.dot(merged.astype(BF16), wout_ref[...], preferred_element_type=F32)
    o_ref[...] = _layer_norm(h, g_ref[...], b_ref[...])


def _merge(x2d, oa, ob, wga, wgb, wpa, wpb, wout, g, b, *, tm):
    m = x2d.shape[0]
    row = lambda w: pl.BlockSpec((tm, w), lambda i: (i, 0))
    full = lambda a: pl.BlockSpec(a.shape, lambda i: (0,) * a.ndim)
    return pl.pallas_call(
        _merge_body,
        grid=(m // tm,),
        in_specs=[row(D_MODEL), row(ATTN_DIM), row(GMLP_DIM), full(wga), full(wgb), full(wpa), full(wpb),
                  full(wout), full(g), full(b)],
        out_specs=row(D_MODEL),
        out_shape=jax.ShapeDtypeStruct((m, D_MODEL), F32),
        compiler_params=_cparams(("parallel",)),
        name="merge",
    )(x2d, oa, ob, wga, wgb, wpa, wpb, wout, g, b)


FF_SPLIT = 4


def _ffn_body(x_ref, w1_ref, w2_ref, g_ref, b_ref, o_ref):
    x = x_ref[...]
    xb = x.astype(BF16)
    cw = D_FF // FF_SPLIT
    ff = jnp.zeros(x.shape, F32)
    for c in range(FF_SPLIT):
        hcol = jnp.maximum(jnp.dot(xb, w1_ref[:, c * cw:(c + 1) * cw], preferred_element_type=F32), 0.0)
        ff = ff + jnp.dot((hcol * hcol).astype(BF16), w2_ref[c * cw:(c + 1) * cw, :],
                          preferred_element_type=F32)
    o_ref[...] = _layer_norm(ALPHA * x + ff, g_ref[...], b_ref[...])


def _ffn(x2d, w1, w2, g, b, *, tm):
    m = x2d.shape[0]
    row = lambda w: pl.BlockSpec((tm, w), lambda i: (i, 0))
    full = lambda a: pl.BlockSpec(a.shape, lambda i: (0,) * a.ndim)
    return pl.pallas_call(
        _ffn_body,
        grid=(m // tm,),
        in_specs=[row(D_MODEL), full(w1), full(w2), full(g), full(b)],
        out_specs=row(D_MODEL),
        out_shape=jax.ShapeDtypeStruct((m, D_MODEL), F32),
        compiler_params=_cparams(("parallel",)),
        name="ffn",
    )(x2d, w1, w2, g, b)


def _rope_tables(pos, rows):
    half = ROPE_DIM // 2
    freqs = ROPE_THETA ** (-jnp.arange(half, dtype=F32) * 2.0 / ROPE_DIM)
    ang = pos.astype(F32)[:, None] * freqs[None, :]
    cos, sin = jnp.cos(ang), jnp.sin(ang)
    t = pos.shape[0]
    rest = HEAD_DIM - ROPE_DIM
    cos_h = jnp.concatenate([cos, cos, jnp.ones((t, rest), F32)], axis=1)
    sn_h = jnp.concatenate([-sin, jnp.zeros((t, half + rest), F32)], axis=1)
    sp_h = jnp.concatenate([jnp.zeros((t, half), F32), sin, jnp.zeros((t, rest), F32)], axis=1)
    reps = (max(rows // t, 1), LANES // HEAD_DIM)
    return tuple(jnp.tile(a, reps) for a in (cos_h, sn_h, sp_h))


def _split_w_in(w):
    sizes = (ATTN_DIM, KV_DIM, KV_DIM, IDX_HEADS * IDX_DIM, IDX_DIM, IDX_HEADS, GMLP_DIM, GMLP_DIM,
             D_MODEL, D_MODEL)
    offs = [0]
    for s in sizes:
        offs.append(offs[-1] + s)
    parts = [w[:, offs[j]:offs[j + 1]] for j in range(len(sizes))]
    q, k, v, qi, ki, wi, u, vg, ga, gb = parts
    pad = jnp.zeros((w.shape[0], LANES - IDX_DIM - IDX_HEADS), w.dtype)
    w_a = jnp.concatenate([q, k, v, qi, ki, wi, pad, u, vg], axis=1).astype(BF16)
    return w_a, ga.astype(BF16), gb.astype(BF16)


def _split_hi_lo(ki):
    hi = ki.astype(BF16)
    lo = (ki - hi.astype(F32)).astype(BF16)
    return jnp.concatenate([hi, hi, lo, lo], axis=-1)


def _pad_keys(a, s_pad):
    return jnp.pad(a, ((0, 0), (0, s_pad - a.shape[1]), (0, 0)))


def kernel(x_prompt, x_sample, cache_k, cache_v, cache_idx_k, w_in, idx_k_g, idx_k_b, sgu_ln_g, sgu_ln_b,
           w_s, b_s, w_pa, w_pb, w_out, ln1_g, ln1_b, w_ff1, w_ff2, ln2_g, ln2_b):
    bp, tp, _ = x_prompt.shape
    bs, ts, _ = x_sample.shape
    depth = w_in.shape[0]
    past = cache_k.shape[2]
    topk_p = min(TOPK_MAX, tp // 4)
    topk_s = min(TOPK_MAX, (past + ts) // 4)
    tm_p = min(512, bp * tp)
    tm_s = bs * ts
    s_all = past + ts
    s_pad = -(-s_all // KEY_TILE) * KEY_TILE

    tabs_p = _rope_tables(jnp.arange(tp), tm_p)
    tabs_s = _rope_tables(past + jnp.arange(ts), tm_s)

    xp = x_prompt.reshape(bp * tp, D_MODEL)
    xs = x_sample.reshape(bs * ts, D_MODEL)
    pk, pv, pik, sk, sv, sik, ssv = [], [], [], [], [], [], []
    row2 = lambda a: a.reshape(1, -1)
    for l in range(depth):
        w_a, w_ga, w_gb = _split_w_in(w_in[l])
        kig = row2(jnp.pad(idx_k_g[l], (0, LANES - IDX_DIM)))
        kib = row2(jnp.pad(idx_k_b[l], (0, LANES - IDX_DIM)))
        sg, sb = row2(sgu_ln_g[l]), row2(sgu_ln_b[l])
        wpa, wpb, wout = w_pa[l].astype(BF16), w_pb[l].astype(BF16), w_out[l].astype(BF16)
        w1, w2 = w_ff1[l].astype(BF16), w_ff2[l].astype(BF16)
        g1, b1, g2, b2 = row2(ln1_g[l]), row2(ln1_b[l]), row2(ln2_g[l]), row2(ln2_b[l])

        qs, kf, vf, kb, vb, qia, kiw, kihl, ob = _inproj(
            xp, w_a, tabs_p, kig, kib, sg, sb, w_s[l], b_s[l].T, tm=tm_p, cl=GMLP_CHUNK, emit_vn=False)
        r3 = lambda a: a.reshape(bp, tp, a.shape[-1])
        oa = _attention(r3(qs), r3(qia), r3(kiw), r3(kihl), r3(kb), r3(vb),
                        qb=128, pos0=0, s_true=tp, topk=topk_p)
        x1 = _merge(xp, oa.reshape(bp * tp, ATTN_DIM), ob, w_ga, w_gb, wpa, wpb, wout, g1, b1, tm=tm_p)
        xp = _ffn(x1, w1, w2, g2, b2, tm=tm_p)
        pk.append(kf.reshape(bp, tp, N_KV_HEADS, HEAD_DIM))
        pv.append(vf.reshape(bp, tp, N_KV_HEADS, HEAD_DIM))
        pik.append(kiw[:, :IDX_DIM].reshape(bp, tp, IDX_DIM))

        qs, kf, vf, kb, vb, qia, kiw, kihl, ob, vn = _inproj(
            xs, w_a, tabs_s, kig, kib, sg, sb, w_s[l][:, :ts, :ts], b_s[l][:, :ts].T,
            tm=tm_s, cl=ts, emit_vn=True)
        r3 = lambda a: a.reshape(bs, ts, a.shape[-1])
        k_all = _pad_keys(jnp.concatenate([cache_k[l].reshape(bs, past, KV_DIM).astype(BF16), r3(kb)], 1), s_pad)
        v_all = _pad_keys(jnp.concatenate([cache_v[l].reshape(bs, past, KV_DIM).astype(BF16), r3(vb)], 1), s_pad)
        ki_all = _pad_keys(jnp.concatenate([_split_hi_lo(cache_idx_k[l]), r3(kihl)], 1), s_pad)
        oa = _attention(r3(qs), r3(qia), r3(kiw), ki_all, k_all, v_all,
                        qb=ts, pos0=past, s_true=s_all, topk=topk_s)
        x1 = _merge(xs, oa.reshape(bs * ts, ATTN_DIM), ob, w_ga, w_gb, wpa, wpb, wout, g1, b1, tm=tm_s)
        xs = _ffn(x1, w1, w2, g2, b2, tm=tm_s)
        sk.append(kf.reshape(bs, ts, N_KV_HEADS, HEAD_DIM))
        sv.append(vf.reshape(bs, ts, N_KV_HEADS, HEAD_DIM))
        sik.append(kiw[:, :IDX_DIM].reshape(bs, ts, IDX_DIM))
        ssv.append(vn.reshape(bs, ts, GMLP_DIM))

    return (xp.reshape(bp, tp, D_MODEL), xs.reshape(bs, ts, D_MODEL),
            jnp.stack(pk), jnp.stack(pv), jnp.stack(pik),
            jnp.stack(sk), jnp.stack(sv), jnp.stack(sik), jnp.stack(ssv))
```

```python
import functools

import jax
import jax.numpy as jnp
from jax import lax
from jax.experimental import pallas as pl
from jax.experimental.pallas import tpu as pltpu

F32 = jnp.float32
BF16 = jnp.bfloat16

D_MODEL = 1024
N_HEADS = 8
N_KV_HEADS = 2
HEAD_DIM = 64
ATTN_DIM = N_HEADS * HEAD_DIM
KV_DIM = N_KV_HEADS * HEAD_DIM
IDX_HEADS = 4
IDX_DIM = 64
TOPK_MAX = 256
CHUNK = 64
GMLP_CHUNK = 128
GMLP_GROUPS = 4
GMLP_DIM = 512
GMLP_GROUP_DIM = GMLP_DIM // GMLP_GROUPS
D_FF = 4 * D_MODEL
ROPE_THETA = 500000.0
ROPE_DIM = HEAD_DIM // 4
DEPTH = 2
ALPHA = (2 * DEPTH) ** 0.25
LN_EPS = 1e-5

LANES = 128
Q_SCALE = HEAD_DIM ** -0.5
IDX_SCALE = IDX_DIM ** -0.5
WI_SCALE = IDX_HEADS ** -0.5
NEG_INF = float("-inf")
M_INIT = -1e30

OFF_Q, OFF_K, OFF_V, OFF_QI, OFF_KIW, OFF_U, OFF_VG = 0, 512, 640, 768, 1024, 1152, 1664
IN_COLS = 2176
KEY_TILE = 512
Q_BLOCK = LANES
SEARCH_CHUNK = 256
VT_ROWS = 80
VMEM_LIMIT = 56 * 1024 * 1024


def _cparams(sem):
    return pltpu.CompilerParams(dimension_semantics=sem, vmem_limit_bytes=VMEM_LIMIT)


def _layer_norm(x, g, b):
    mu = jnp.mean(x, axis=-1, keepdims=True)
    d = x - mu
    var = jnp.mean(d * d, axis=-1, keepdims=True)
    return d * lax.rsqrt(var + LN_EPS) * g + b


def _rope(y, cos, sn, sp):
    w = y.shape[-1]
    half = ROPE_DIM // 2
    return y * cos + pltpu.roll(y, w - half, 1) * sn + pltpu.roll(y, half, 1) * sp


def _tile_lanes(a, n):
    return a if n == 1 else jnp.concatenate([a] * n, axis=1)


def _inproj_body(x_ref, w_ref, cos_ref, sn_ref, sp_ref, kig_ref, kib_ref, sg_ref, sb_ref,
                 ws_ref, bst_ref,
                 qp_ref, kf_ref, vf_ref, kb_ref, vb_ref, qia_ref, kiw_ref, kihl_ref, ob_ref,
                 *vn_refs, cl, v_transposed):
    tm = x_ref.shape[0]
    xb = x_ref[...].astype(BF16)

    def proj(lo, n):
        return jnp.dot(xb, w_ref[:, lo:lo + n], preferred_element_type=F32)

    cos, sn, sp = cos_ref[...], sn_ref[...], sp_ref[...]
    lane = lax.broadcasted_iota(jnp.int32, (tm, LANES), 1)
    left = lane < IDX_DIM

    q = _rope(proj(OFF_Q, ATTN_DIM), _tile_lanes(cos, 4), _tile_lanes(sn, 4), _tile_lanes(sp, 4)) * Q_SCALE
    heads_per_kv = N_HEADS // N_KV_HEADS
    for j in range(ATTN_DIM // LANES):
        tile = q[:, j * LANES:(j + 1) * LANES]
        rolled = pltpu.roll(tile, HEAD_DIM, 1)
        on_left = (2 * j) // heads_per_kv == 0
        keep = left if on_left else jnp.logical_not(left)
        even = jnp.where(keep, tile if on_left else rolled, 0.0)
        odd = jnp.where(keep, rolled if on_left else tile, 0.0)
        qp_ref[:, (2 * j) * LANES:(2 * j + 1) * LANES] = even.astype(BF16)
        qp_ref[:, (2 * j + 1) * LANES:(2 * j + 2) * LANES] = odd.astype(BF16)

    k = _rope(proj(OFF_K, KV_DIM), cos, sn, sp)
    kf_ref[...] = k
    kb_ref[...] = k.astype(BF16)
    v = proj(OFF_V, KV_DIM)
    vf_ref[...] = v
    vb = v.astype(BF16)
    if v_transposed:
        r_e = lax.broadcasted_iota(jnp.int32, (N_KV_HEADS * VT_ROWS, LANES), 0)
        c_e = lax.broadcasted_iota(jnp.int32, (N_KV_HEADS * VT_ROWS, LANES), 1)
        head, d = r_e // VT_ROWS, r_e % VT_ROWS
        sel = jnp.where((d < HEAD_DIM) & (c_e == head * HEAD_DIM + d), 1.0, 0.0).astype(BF16)
        vt = lax.dot_general(sel, vb, (((1,), (1,)), ((), ())), preferred_element_type=F32)
        r_o = lax.broadcasted_iota(jnp.int32, (N_KV_HEADS * VT_ROWS, tm), 0)
        vb_ref[...] = jnp.where(r_o % VT_ROWS == HEAD_DIM, 1.0, vt).astype(BF16)
    else:
        vb_ref[...] = vb

    qi = _rope(proj(OFF_QI, IDX_HEADS * IDX_DIM), _tile_lanes(cos, 2), _tile_lanes(sn, 2),
               _tile_lanes(sp, 2)) * IDX_SCALE
    qi_hi = qi.astype(BF16).astype(F32)
    qi_lo = qi - qi_hi
    for t in range(IDX_HEADS // 2):
        h_t = qi_hi[:, t * LANES:(t + 1) * LANES]
        l_t = qi_lo[:, t * LANES:(t + 1) * LANES]
        a0 = jnp.where(left, h_t, pltpu.roll(l_t, IDX_DIM, 1))
        a1 = jnp.where(left, pltpu.roll(h_t, IDX_DIM, 1), l_t)
        for j, a in enumerate((a0, a1)):
            base = (2 * t + j) * 2 * LANES
            qia_ref[:, base:base + 2 * LANES] = jnp.concatenate([a, a], axis=1).astype(BF16)

    slab = proj(OFF_KIW, LANES)
    mu = jnp.sum(jnp.where(left, slab, 0.0), axis=1, keepdims=True) * (1.0 / IDX_DIM)
    d = jnp.where(left, slab - mu, 0.0)
    var = jnp.sum(d * d, axis=1, keepdims=True) * (1.0 / IDX_DIM)
    kin = d * lax.rsqrt(var + LN_EPS) * kig_ref[...] + kib_ref[...]
    slab = jnp.where(left, kin, slab)
    slab = _rope(slab, jnp.where(left, cos, 1.0), jnp.where(left, sn, 0.0), jnp.where(left, sp, 0.0))
    kiw_ref[...] = slab
    kz = jnp.where(left, slab, 0.0)
    k_hi = kz.astype(BF16).astype(F32)
    k_lo = kz - k_hi
    hh = k_hi + pltpu.roll(k_hi, IDX_DIM, 1)
    ll = k_lo + pltpu.roll(k_lo, IDX_DIM, 1)
    kihl_ref[...] = jnp.concatenate([hh, ll], axis=1).astype(BF16)

    u = proj(OFF_U, GMLP_DIM)
    vn = _layer_norm(proj(OFF_VG, GMLP_DIM), sg_ref[...], sb_ref[...])
    if vn_refs:
        vn_refs[0][...] = vn
    vnb = vn.astype(BF16)
    r_i = lax.broadcasted_iota(jnp.int32, (cl, cl), 0)
    c_i = lax.broadcasted_iota(jnp.int32, (cl, cl), 1)
    for g in range(GMLP_GROUPS):
        wg = jnp.where(r_i >= c_i, ws_ref[g], 0.0).astype(BF16)
        bg = bst_ref[:, g:g + 1]
        gs = slice(g * GMLP_GROUP_DIM, (g + 1) * GMLP_GROUP_DIM)
        for c in range(tm // cl):
            rs = slice(c * cl, (c + 1) * cl)
            mix = jnp.dot(wg, vnb[rs, gs], preferred_element_type=F32) + bg
            ob_ref[rs, gs] = (u[rs, gs] * mix).astype(BF16)


def _inproj(x2d, w_a, tabs, kig, kib, sg, sb, ws, bst, *, tm, cl, emit_vn, v_transposed):
    m = x2d.shape[0]
    cos, sn, sp = tabs
    nt = cos.shape[0] // tm
    row = lambda w: pl.BlockSpec((tm, w), lambda i: (i, 0))
    tab = pl.BlockSpec((tm, LANES), lambda i: (i % nt, 0))
    full = lambda a: pl.BlockSpec(a.shape, lambda i: (0,) * a.ndim)
    if v_transposed:
        vb_shape = jax.ShapeDtypeStruct((m // tm, N_KV_HEADS * VT_ROWS, tm), BF16)
        vb_spec = pl.BlockSpec((None, N_KV_HEADS * VT_ROWS, tm), lambda i: (i, 0, 0))
    else:
        vb_shape, vb_spec = jax.ShapeDtypeStruct((m, KV_DIM), BF16), row(KV_DIM)
    out_shapes = [
        jax.ShapeDtypeStruct((m, N_HEADS * LANES), BF16),
        jax.ShapeDtypeStruct((m, KV_DIM), F32),
        jax.ShapeDtypeStruct((m, KV_DIM), F32),
        jax.ShapeDtypeStruct((m, KV_DIM), BF16),
        vb_shape,
        jax.ShapeDtypeStruct((m, IDX_HEADS * 4 * IDX_DIM), BF16),
        jax.ShapeDtypeStruct((m, LANES), F32),
        jax.ShapeDtypeStruct((m, 4 * IDX_DIM), BF16),
        jax.ShapeDtypeStruct((m, GMLP_DIM), BF16),
    ]
    out_specs = [row(N_HEADS * LANES), row(KV_DIM), row(KV_DIM), row(KV_DIM), vb_spec,
                 row(IDX_HEADS * 4 * IDX_DIM), row(LANES), row(4 * IDX_DIM), row(GMLP_DIM)]
    if emit_vn:
        out_shapes.append(jax.ShapeDtypeStruct((m, GMLP_DIM), F32))
        out_specs.append(row(GMLP_DIM))
    return pl.pallas_call(
        functools.partial(_inproj_body, cl=cl, v_transposed=v_transposed),
        grid=(m // tm,),
        in_specs=[row(D_MODEL), full(w_a), tab, tab, tab, full(kig), full(kib), full(sg), full(sb),
                  full(ws), full(bst)],
        out_specs=out_specs,
        out_shape=out_shapes,
        compiler_params=_cparams(("parallel",)),
        name="inproj",
    )(x2d, w_a, cos, sn, sp, kig, kib, sg, sb, ws, bst)


def _nt_dot(a, b):
    return lax.dot_general(a, b, (((1,), (1,)), ((), ())), preferred_element_type=F32)


def _tree_sum(parts):
    while len(parts) > 1:
        parts = [parts[j] + parts[j + 1] for j in range(0, len(parts) - 1, 2)] + (
            [parts[-1]] if len(parts) % 2 else [])
    return parts[0]


def _attn_body(wit_ref, qp_ref, qia_ref, kihl_ref, kb_ref, vt_ref, o_ref,
               keys_ref, bias_ref, s_ref, *, qb, pos0, s_true, topk):
    i = pl.program_id(1)
    sc_n = SEARCH_CHUNK
    cpt = KEY_TILE // sc_n
    grp = N_HEADS // N_KV_HEADS
    qlane = lax.broadcasted_iota(jnp.int32, (1, qb), 1)
    qpos = pos0 + i * qb + qlane
    limit = jnp.minimum((qpos // CHUNK + 1) * CHUNK, s_true)
    kv_len = jnp.minimum(((pos0 + i * qb + qb - 1) // CHUNK + 1) * CHUNK, s_true)
    n_kt = (kv_len + KEY_TILE - 1) // KEY_TILE
    n_c = (kv_len + sc_n - 1) // sc_n
    krow = lax.broadcasted_iota(jnp.int32, (sc_n, 1), 0)

    wit = wit_ref[...] * WI_SCALE
    hw = 4 * IDX_DIM
    pairs = [jnp.concatenate([qia_ref[:, (2 * hp) * hw:(2 * hp + 1) * hw],
                              qia_ref[:, (2 * hp + 1) * hw:(2 * hp + 2) * hw]], axis=0)
             for hp in range(IDX_HEADS // 2)]

    def score_tile(t, carry):
        kt = kihl_ref[pl.ds(pl.multiple_of(t * KEY_TILE, KEY_TILE), KEY_TILE), :]
        sc = jnp.zeros((KEY_TILE, qb), F32)
        for hp, a2 in enumerate(pairs):
            dots = jnp.maximum(_nt_dot(kt, a2), 0.0)
            sc = sc + wit[2 * hp:2 * hp + 1, :] * dots[:, :qb] + wit[2 * hp + 1:2 * hp + 2, :] * dots[:, qb:]
        for j in range(cpt):
            kidx = t * KEY_TILE + j * sc_n + krow
            s_j = jnp.where(kidx < limit, sc[j * sc_n:(j + 1) * sc_n], NEG_INF)
            bits = lax.bitcast_convert_type(s_j, jnp.int32)
            keys_ref[t * cpt + j] = bits ^ ((bits >> 31) & 0x7FFFFFFF)
        return carry

    lax.fori_loop(0, n_kt, score_tile, 0)

    n_acc = 4

    def count(pred, thr):
        def body(c, accs):
            ones = jnp.where(pred(keys_ref[c], thr), 1.0, 0.0)
            rows = [ones[8 * j:8 * j + 8] for j in range(sc_n // 8)]
            per = len(rows) // n_acc
            return tuple(a + _tree_sum(rows[k * per:(k + 1) * per]) for k, a in enumerate(accs))
        accs = lax.fori_loop(0, n_c, body, tuple(jnp.zeros((8, qb), F32) for _ in range(n_acc)))
        return jnp.sum(_tree_sum(list(accs)), axis=0, keepdims=True)

    def bit_step(it, carry):
        thr, best = carry
        cand = thr + jnp.left_shift(jnp.int32(1), 31 - it)
        cnt = count(lambda kk, t_: kk >= t_, cand)
        ok = cnt >= topk
        return jnp.where(ok, cand, thr), jnp.where(ok, cnt, best)

    thr, n_ge = lax.fori_loop(0, 32, bit_step,
                              (jnp.full((1, qb), -2 ** 31, jnp.int32), jnp.zeros((1, qb), F32)))

    overflow = jnp.max(n_ge) > topk

    @pl.when(jnp.logical_not(overflow))
    def _():
        def body(c, carry):
            sel = (keys_ref[c] >= thr) & (c * sc_n + krow < limit)
            bias_ref[c] = jnp.where(sel, 0.0, NEG_INF)
            return carry
        lax.fori_loop(0, n_kt * cpt, body, 0)

    @pl.when(overflow)
    def _():
        need = topk - count(lambda kk, t_: kk > t_, thr)
        r_i = lax.broadcasted_iota(jnp.int32, (sc_n, sc_n), 0)
        c_i = lax.broadcasted_iota(jnp.int32, (sc_n, sc_n), 1)
        lower = jnp.where(c_i <= r_i, 1.0, 0.0).astype(BF16)

        def tie_chunk(c, off):
            kk = keys_ref[c]
            eq = kk == thr
            local = jnp.dot(lower, jnp.where(eq, 1.0, 0.0).astype(BF16), preferred_element_type=F32)
            take = eq & (local + off <= need)
            sel = ((kk > thr) | take) & (c * sc_n + krow < limit)
            bias_ref[c] = jnp.where(sel, 0.0, NEG_INF)
            return off + local[sc_n - 1:sc_n, :]

        lax.fori_loop(0, n_kt * cpt, tie_chunk, jnp.zeros((1, qb), F32))

    r_e = lax.broadcasted_iota(jnp.int32, (qb, qb), 0)
    c_e = lax.broadcasted_iota(jnp.int32, (qb, qb), 1)
    eye = jnp.where(r_e == c_e, 1.0, 0.0).astype(BF16)
    kv_heads = range(N_KV_HEADS)
    qns = [jnp.concatenate(
        [qp_ref[:, (grp * n + g) * LANES:(grp * n + g + 1) * LANES] for g in range(grp)], axis=0)
        for n in kv_heads]

    def pass_a(tiles, ms):
        ms = list(ms)
        for t in tiles:
            kt = kb_ref[pl.ds(pl.multiple_of(t * KEY_TILE, KEY_TILE), KEY_TILE), :]
            bias = jnp.concatenate([bias_ref[t * cpt + j] for j in range(cpt)], axis=0)
            bias = jnp.concatenate([bias] * grp, axis=1)
            for n in kv_heads:
                s = _nt_dot(kt, qns[n]) + bias
                s_ref[n, t] = s
                ms[n] = jnp.maximum(ms[n], jnp.max(s, axis=0, keepdims=True))
        return tuple(ms)

    def pass_b(tiles, accs, ms):
        accs = list(accs)
        for t in tiles:
            for n in kv_heads:
                p = jnp.exp(s_ref[n, t] - ms[n]).astype(BF16)
                vt = vt_ref[t, n * VT_ROWS:(n + 1) * VT_ROWS, :]
                accs[n] = accs[n] + jnp.dot(vt, p, preferred_element_type=F32)
        return tuple(accs)

    n_pair = n_kt // 2
    odd = n_kt % 2 == 1
    ms = tuple(jnp.full((1, grp * qb), M_INIT, F32) for _ in kv_heads)
    ms = lax.fori_loop(0, n_pair, lambda pr, c: pass_a((2 * pr, 2 * pr + 1), c), ms)
    ms = lax.cond(odd, lambda c: pass_a((n_kt - 1,), c), lambda c: c, ms)
    accs = tuple(jnp.zeros((VT_ROWS, grp * qb), F32) for _ in kv_heads)
    accs = lax.fori_loop(0, n_pair, lambda pr, c: pass_b((2 * pr, 2 * pr + 1), c, ms), accs)
    accs = lax.cond(odd, lambda c: pass_b((n_kt - 1,), c, ms), lambda c: c, accs)

    for n in kv_heads:
        acc = accs[n]
        ot = (acc[:HEAD_DIM] / acc[HEAD_DIM:HEAD_DIM + 1]).astype(BF16)
        for g in range(grp):
            h = grp * n + g
            o_ref[:, h * HEAD_DIM:(h + 1) * HEAD_DIM] = _nt_dot(
                eye, ot[:, g * qb:(g + 1) * qb]).astype(o_ref.dtype)


def _attention(wit, qp, qia, kihl, kb, vt, *, qb, pos0, s_true, topk):
    b, t, _ = qp.shape
    s_pad = kihl.shape[1]
    assert t % qb == 0 and s_pad % KEY_TILE == 0
    grp = N_HEADS // N_KV_HEADS
    qblk = lambda w: pl.BlockSpec((None, qb, w), lambda bi, i: (bi, i, 0))
    kblk = lambda w: pl.BlockSpec((None, s_pad, w), lambda bi, i: (bi, 0, 0))
    return pl.pallas_call(
        functools.partial(_attn_body, qb=qb, pos0=pos0, s_true=s_true, topk=topk),
        grid=(b, t // qb),
        in_specs=[pl.BlockSpec((None, None, 8, qb), lambda bi, i: (bi, i, 0, 0)),
                  qblk(N_HEADS * LANES), qblk(IDX_HEADS * 4 * IDX_DIM),
                  kblk(4 * IDX_DIM), kblk(KV_DIM),
                  pl.BlockSpec((None, s_pad // KEY_TILE, N_KV_HEADS * VT_ROWS, KEY_TILE),
                               lambda bi, i: (bi, 0, 0, 0))],
        out_specs=qblk(ATTN_DIM),
        out_shape=jax.ShapeDtypeStruct((b, t, ATTN_DIM), BF16),
        scratch_shapes=[
            pltpu.VMEM((s_pad // SEARCH_CHUNK, SEARCH_CHUNK, qb), jnp.int32),
            pltpu.VMEM((s_pad // SEARCH_CHUNK, SEARCH_CHUNK, qb), F32),
            pltpu.VMEM((N_KV_HEADS, s_pad // KEY_TILE, KEY_TILE, grp * qb), F32),
        ],
        compiler_params=_cparams(("parallel", "arbitrary")),
        name="sparse_attn",
    )(wit, qp, qia, kihl, kb, vt)


def _merge_body(x_ref, oa_ref, ob_ref, wga_ref, wgb_ref, wpa_ref, wpb_ref, wout_ref, g_ref, b_ref, o_ref):
    x = x_ref[...]
    xb = x.astype(BF16)
    ga = jnp.dot(xb, wga_ref[...], preferred_element_type=F32)
    gb = jnp.dot(xb, wgb_ref[...], preferred_element_type=F32)
    pa = jnp.dot(oa_ref[...], wpa_ref[...], preferred_element_type=F32)
    pb = jnp.dot(ob_ref[...], wpb_ref[...], preferred_element_type=F32)
    merged = jax.nn.sigmoid(ga) * pa + jax.nn.sigmoid(gb) * pb
    h = ALPHA * x + jnp.dot(merged.astype(BF16), wout_ref[...], preferred_element_type=F32)
    o_ref[...] = _layer_norm(h, g_ref[...], b_ref[...])


def _merge(x2d, oa, ob, wga, wgb, wpa, wpb, wout, g, b, *, tm):
    m = x2d.shape[0]
    row = lambda w: pl.BlockSpec((tm, w), lambda i: (i, 0))
    full = lambda a: pl.BlockSpec(a.shape, lambda i: (0,) * a.ndim)
    return pl.pallas_call(
        _merge_body,
        grid=(m // tm,),
        in_specs=[row(D_MODEL), row(ATTN_DIM), row(GMLP_DIM), full(wga), full(wgb), full(wpa), full(wpb),
                  full(wout), full(g), full(b)],
        out_specs=row(D_MODEL),
        out_shape=jax.ShapeDtypeStruct((m, D_MODEL), F32),
        compiler_params=_cparams(("parallel",)),
        name="merge",
    )(x2d, oa, ob, wga, wgb, wpa, wpb, wout, g, b)


FF_SPLIT = 4


def _ffn_body(x_ref, w1_ref, w2_ref, g_ref, b_ref, o_ref):
    x = x_ref[...]
    xb = x.astype(BF16)
    cw = D_FF // FF_SPLIT
    ff = jnp.zeros(x.shape, F32)
    for c in range(FF_SPLIT):
        hcol = jnp.maximum(jnp.dot(xb, w1_ref[:, c * cw:(c + 1) * cw], preferred_element_type=F32), 0.0)
        ff = ff + jnp.dot((hcol * hcol).astype(BF16), w2_ref[c * cw:(c + 1) * cw, :],
                          preferred_element_type=F32)
    o_ref[...] = _layer_norm(ALPHA * x + ff, g_ref[...], b_ref[...])


def _ffn(x2d, w1, w2, g, b, *, tm):
    m = x2d.shape[0]
    row = lambda w: pl.BlockSpec((tm, w), lambda i: (i, 0))
    full = lambda a: pl.BlockSpec(a.shape, lambda i: (0,) * a.ndim)
    return pl.pallas_call(
        _ffn_body,
        grid=(m // tm,),
        in_specs=[row(D_MODEL), full(w1), full(w2), full(g), full(b)],
        out_specs=row(D_MODEL),
        out_shape=jax.ShapeDtypeStruct((m, D_MODEL), F32),
        compiler_params=_cparams(("parallel",)),
        name="ffn",
    )(x2d, w1, w2, g, b)


def _rope_tables(pos, rows):
    half = ROPE_DIM // 2
    freqs = ROPE_THETA ** (-jnp.arange(half, dtype=F32) * 2.0 / ROPE_DIM)
    ang = pos.astype(F32)[:, None] * freqs[None, :]
    cos, sin = jnp.cos(ang), jnp.sin(ang)
    t = pos.shape[0]
    rest = HEAD_DIM - ROPE_DIM
    cos_h = jnp.concatenate([cos, cos, jnp.ones((t, rest), F32)], axis=1)
    sn_h = jnp.concatenate([-sin, jnp.zeros((t, half + rest), F32)], axis=1)
    sp_h = jnp.concatenate([jnp.zeros((t, half), F32), sin, jnp.zeros((t, rest), F32)], axis=1)
    reps = (max(rows // t, 1), LANES // HEAD_DIM)
    return tuple(jnp.tile(a, reps) for a in (cos_h, sn_h, sp_h))


def _split_w_in(w):
    sizes = (ATTN_DIM, KV_DIM, KV_DIM, IDX_HEADS * IDX_DIM, IDX_DIM, IDX_HEADS, GMLP_DIM, GMLP_DIM,
             D_MODEL, D_MODEL)
    offs = [0]
    for s in sizes:
        offs.append(offs[-1] + s)
    parts = [w[:, offs[j]:offs[j + 1]] for j in range(len(sizes))]
    q, k, v, qi, ki, wi, u, vg, ga, gb = parts
    pad = jnp.zeros((w.shape[0], LANES - IDX_DIM - IDX_HEADS), w.dtype)
    w_a = jnp.concatenate([q, k, v, qi, ki, wi, pad, u, vg], axis=1).astype(BF16)
    return w_a, ga.astype(BF16), gb.astype(BF16)


def _split_hi_lo(ki):
    hi = ki.astype(BF16)
    lo = (ki - hi.astype(F32)).astype(BF16)
    return jnp.concatenate([hi, hi, lo, lo], axis=-1)


def _head_weights_t(kiw):
    b, t, _ = kiw.shape
    wi = kiw[:, :, IDX_DIM:IDX_DIM + IDX_HEADS].reshape(b, t // Q_BLOCK, Q_BLOCK, IDX_HEADS)
    return jnp.pad(wi.transpose(0, 1, 3, 2), ((0, 0), (0, 0), (0, 8 - IDX_HEADS), (0, 0)))


def _augment_vt(vt):
    b, nt, _, kt = vt.shape
    v4 = vt.reshape(b, nt, N_KV_HEADS, HEAD_DIM, kt)
    ones = jnp.ones((b, nt, N_KV_HEADS, 1, kt), vt.dtype)
    zeros = jnp.zeros((b, nt, N_KV_HEADS, VT_ROWS - HEAD_DIM - 1, kt), vt.dtype)
    return jnp.concatenate([v4, ones, zeros], axis=3).reshape(b, nt, N_KV_HEADS * VT_ROWS, kt)


def _pad_keys(a, s_pad):
    return jnp.pad(a, ((0, 0), (0, s_pad - a.shape[1]), (0, 0)))


def kernel(x_prompt, x_sample, cache_k, cache_v, cache_idx_k, w_in, idx_k_g, idx_k_b, sgu_ln_g, sgu_ln_b,
           w_s, b_s, w_pa, w_pb, w_out, ln1_g, ln1_b, w_ff1, w_ff2, ln2_g, ln2_b):
    bp, tp, _ = x_prompt.shape
    bs, ts, _ = x_sample.shape
    depth = w_in.shape[0]
    past = cache_k.shape[2]
    topk_p = min(TOPK_MAX, tp // 4)
    topk_s = min(TOPK_MAX, (past + ts) // 4)
    tm_p = min(512, bp * tp)
    tm_s = bs * ts
    s_all = past + ts
    s_pad = -(-s_all // KEY_TILE) * KEY_TILE

    tabs_p = _rope_tables(jnp.arange(tp), tm_p)
    tabs_s = _rope_tables(past + jnp.arange(ts), tm_s)

    xp = x_prompt.reshape(bp * tp, D_MODEL)
    xs = x_sample.reshape(bs * ts, D_MODEL)
    pk, pv, pik, sk, sv, sik, ssv = [], [], [], [], [], [], []
    row2 = lambda a: a.reshape(1, -1)
    for l in range(depth):
        w_a, w_ga, w_gb = _split_w_in(w_in[l])
        kig = row2(jnp.pad(idx_k_g[l], (0, LANES - IDX_DIM)))
        kib = row2(jnp.pad(idx_k_b[l], (0, LANES - IDX_DIM)))
        sg, sb = row2(sgu_ln_g[l]), row2(sgu_ln_b[l])
        wpa, wpb, wout = w_pa[l].astype(BF16), w_pb[l].astype(BF16), w_out[l].astype(BF16)
        w1, w2 = w_ff1[l].astype(BF16), w_ff2[l].astype(BF16)
        g1, b1, g2, b2 = row2(ln1_g[l]), row2(ln1_b[l]), row2(ln2_g[l]), row2(ln2_b[l])

        qp, kf, vf, kb, vt, qia, kiw, kihl, ob = _inproj(
            xp, w_a, tabs_p, kig, kib, sg, sb, w_s[l], b_s[l].T, tm=tm_p, cl=GMLP_CHUNK, emit_vn=False,
            v_transposed=True)
        r3 = lambda a: a.reshape(bp, tp, a.shape[-1])
        oa = _attention(_head_weights_t(r3(kiw)), r3(qp), r3(qia), r3(kihl), r3(kb),
                        vt.reshape(bp, tp // KEY_TILE, N_KV_HEADS * VT_ROWS, KEY_TILE),
                        qb=Q_BLOCK, pos0=0, s_true=tp, topk=topk_p)
        x1 = _merge(xp, oa.reshape(bp * tp, ATTN_DIM), ob, w_ga, w_gb, wpa, wpb, wout, g1, b1, tm=tm_p)
        xp = _ffn(x1, w1, w2, g2, b2, tm=tm_p)
        pk.append(kf.reshape(bp, tp, N_KV_HEADS, HEAD_DIM))
        pv.append(vf.reshape(bp, tp, N_KV_HEADS, HEAD_DIM))
        pik.append(kiw[:, :IDX_DIM].reshape(bp, tp, IDX_DIM))

        qp, kf, vf, kb, vb, qia, kiw, kihl, ob, vn = _inproj(
            xs, w_a, tabs_s, kig, kib, sg, sb, w_s[l][:, :ts, :ts], b_s[l][:, :ts].T,
            tm=tm_s, cl=ts, emit_vn=True, v_transposed=False)
        r3 = lambda a: a.reshape(bs, ts, a.shape[-1])
        k_all = _pad_keys(jnp.concatenate([cache_k[l].reshape(bs, past, KV_DIM).astype(BF16), r3(kb)], 1), s_pad)
        v_all = _pad_keys(jnp.concatenate([cache_v[l].reshape(bs, past, KV_DIM).astype(BF16), r3(vb)], 1), s_pad)
        vt_all = _augment_vt(v_all.reshape(bs, s_pad // KEY_TILE, KEY_TILE, KV_DIM).transpose(0, 1, 3, 2))
        ki_all = _pad_keys(jnp.concatenate([_split_hi_lo(cache_idx_k[l]), r3(kihl)], 1), s_pad)
        padq = lambda a: jnp.pad(r3(a), ((0, 0), (0, Q_BLOCK - ts), (0, 0)))
        oa = _attention(_head_weights_t(padq(kiw)), padq(qp), padq(qia), ki_all, k_all, vt_all,
                        qb=Q_BLOCK, pos0=past, s_true=s_all, topk=topk_s)[:, :ts]
        x1 = _merge(xs, oa.reshape(bs * ts, ATTN_DIM), ob, w_ga, w_gb, wpa, wpb, wout, g1, b1, tm=tm_s)
        xs = _ffn(x1, w1, w2, g2, b2, tm=tm_s)
        sk.append(kf.reshape(bs, ts, N_KV_HEADS, HEAD_DIM))
        sv.append(vf.reshape(bs, ts, N_KV_HEADS, HEAD_DIM))
        sik.append(kiw[:, :IDX_DIM].reshape(bs, ts, IDX_DIM))
        ssv.append(vn.reshape(bs, ts, GMLP_DIM))

    return (xp.reshape(bp, tp, D_MODEL), xs.reshape(bs, ts, D_MODEL),
            jnp.stack(pk), jnp.stack(pv), jnp.stack(pik),
            jnp.stack(sk), jnp.stack(sv), jnp.stack(sik), jnp.stack(ssv))
```

```python
import functools

import jax
import jax.numpy as jnp
from jax import lax
from jax.experimental import pallas as pl
from jax.experimental.pallas import tpu as pltpu

F32 = jnp.float32
BF16 = jnp.bfloat16

D_MODEL = 1024
N_HEADS = 8
N_KV_HEADS = 2
HEAD_DIM = 64
ATTN_DIM = N_HEADS * HEAD_DIM
KV_DIM = N_KV_HEADS * HEAD_DIM
IDX_HEADS = 4
IDX_DIM = 64
TOPK_MAX = 256
CHUNK = 64
GMLP_CHUNK = 128
GMLP_GROUPS = 4
GMLP_DIM = 512
GMLP_GROUP_DIM = GMLP_DIM // GMLP_GROUPS
D_FF = 4 * D_MODEL
ROPE_THETA = 500000.0
ROPE_DIM = HEAD_DIM // 4
DEPTH = 2
ALPHA = (2 * DEPTH) ** 0.25
LN_EPS = 1e-5

LANES = 128
Q_SCALE = HEAD_DIM ** -0.5
IDX_SCALE = IDX_DIM ** -0.5
WI_SCALE = IDX_HEADS ** -0.5
NEG_INF = float("-inf")
M_INIT = -1e30

OFF_Q, OFF_K, OFF_V, OFF_QI, OFF_KIW, OFF_U, OFF_VG = 0, 512, 640, 768, 1024, 1152, 1664
IN_COLS = 2176
KEY_TILE = 512
Q_BLOCK = LANES
SEARCH_CHUNK = 256
VT_ROWS = 80
VMEM_LIMIT = 56 * 1024 * 1024


def _cparams(sem):
    return pltpu.CompilerParams(dimension_semantics=sem, vmem_limit_bytes=VMEM_LIMIT)


def _layer_norm(x, g, b):
    mu = jnp.mean(x, axis=-1, keepdims=True)
    d = x - mu
    var = jnp.mean(d * d, axis=-1, keepdims=True)
    return d * lax.rsqrt(var + LN_EPS) * g + b


def _rope(y, cos, sn, sp):
    w = y.shape[-1]
    half = ROPE_DIM // 2
    return y * cos + pltpu.roll(y, w - half, 1) * sn + pltpu.roll(y, half, 1) * sp


def _tile_lanes(a, n):
    return a if n == 1 else jnp.concatenate([a] * n, axis=1)


def _inproj_body(x_ref, w_ref, cos_ref, sn_ref, sp_ref, kig_ref, kib_ref, sg_ref, sb_ref,
                 ws_ref, bst_ref,
                 qp_ref, kf_ref, vf_ref, kb_ref, vb_ref, qia_ref, kiw_ref, kihl_ref, ob_ref,
                 *vn_refs, cl, v_transposed):
    tm = x_ref.shape[0]
    xb = x_ref[...].astype(BF16)

    def proj(lo, n):
        return jnp.dot(xb, w_ref[:, lo:lo + n], preferred_element_type=F32)

    cos, sn, sp = cos_ref[...], sn_ref[...], sp_ref[...]
    lane = lax.broadcasted_iota(jnp.int32, (tm, LANES), 1)
    left = lane < IDX_DIM

    q = _rope(proj(OFF_Q, ATTN_DIM), _tile_lanes(cos, 4), _tile_lanes(sn, 4), _tile_lanes(sp, 4)) * Q_SCALE
    heads_per_kv = N_HEADS // N_KV_HEADS
    for j in range(ATTN_DIM // LANES):
        tile = q[:, j * LANES:(j + 1) * LANES]
        rolled = pltpu.roll(tile, HEAD_DIM, 1)
        on_left = (2 * j) // heads_per_kv == 0
        keep = left if on_left else jnp.logical_not(left)
        even = jnp.where(keep, tile if on_left else rolled, 0.0)
        odd = jnp.where(keep, rolled if on_left else tile, 0.0)
        qp_ref[:, (2 * j) * LANES:(2 * j + 1) * LANES] = even.astype(BF16)
        qp_ref[:, (2 * j + 1) * LANES:(2 * j + 2) * LANES] = odd.astype(BF16)

    k = _rope(proj(OFF_K, KV_DIM), cos, sn, sp)
    kf_ref[...] = k
    kb_ref[...] = k.astype(BF16)
    v = proj(OFF_V, KV_DIM)
    vf_ref[...] = v
    vb = v.astype(BF16)
    if v_transposed:
        r_e = lax.broadcasted_iota(jnp.int32, (N_KV_HEADS * VT_ROWS, LANES), 0)
        c_e = lax.broadcasted_iota(jnp.int32, (N_KV_HEADS * VT_ROWS, LANES), 1)
        head, d = r_e // VT_ROWS, r_e % VT_ROWS
        sel = jnp.where((d < HEAD_DIM) & (c_e == head * HEAD_DIM + d), 1.0, 0.0).astype(BF16)
        vt = lax.dot_general(sel, vb, (((1,), (1,)), ((), ())), preferred_element_type=F32)
        r_o = lax.broadcasted_iota(jnp.int32, (N_KV_HEADS * VT_ROWS, tm), 0)
        vb_ref[...] = jnp.where(r_o % VT_ROWS == HEAD_DIM, 1.0, vt).astype(BF16)
    else:
        vb_ref[...] = vb

    qi = _rope(proj(OFF_QI, IDX_HEADS * IDX_DIM), _tile_lanes(cos, 2), _tile_lanes(sn, 2),
               _tile_lanes(sp, 2)) * IDX_SCALE
    qi_hi = qi.astype(BF16).astype(F32)
    qi_lo = qi - qi_hi
    for t in range(IDX_HEADS // 2):
        h_t = qi_hi[:, t * LANES:(t + 1) * LANES]
        l_t = qi_lo[:, t * LANES:(t + 1) * LANES]
        a0 = jnp.where(left, h_t, pltpu.roll(l_t, IDX_DIM, 1))
        a1 = jnp.where(left, pltpu.roll(h_t, IDX_DIM, 1), l_t)
        for j, a in enumerate((a0, a1)):
            base = (2 * t + j) * 2 * LANES
            qia_ref[:, base:base + 2 * LANES] = jnp.concatenate([a, a], axis=1).astype(BF16)

    slab = proj(OFF_KIW, LANES)
    mu = jnp.sum(jnp.where(left, slab, 0.0), axis=1, keepdims=True) * (1.0 / IDX_DIM)
    d = jnp.where(left, slab - mu, 0.0)
    var = jnp.sum(d * d, axis=1, keepdims=True) * (1.0 / IDX_DIM)
    kin = d * lax.rsqrt(var + LN_EPS) * kig_ref[...] + kib_ref[...]
    slab = jnp.where(left, kin, slab)
    slab = _rope(slab, jnp.where(left, cos, 1.0), jnp.where(left, sn, 0.0), jnp.where(left, sp, 0.0))
    kiw_ref[...] = slab
    kz = jnp.where(left, slab, 0.0)
    k_hi = kz.astype(BF16).astype(F32)
    k_lo = kz - k_hi
    hh = k_hi + pltpu.roll(k_hi, IDX_DIM, 1)
    ll = k_lo + pltpu.roll(k_lo, IDX_DIM, 1)
    kihl_ref[...] = jnp.concatenate([hh, ll], axis=1).astype(BF16)

    u = proj(OFF_U, GMLP_DIM)
    vn = _layer_norm(proj(OFF_VG, GMLP_DIM), sg_ref[...], sb_ref[...])
    if vn_refs:
        vn_refs[0][...] = vn
    vnb = vn.astype(BF16)
    r_i = lax.broadcasted_iota(jnp.int32, (cl, cl), 0)
    c_i = lax.broadcasted_iota(jnp.int32, (cl, cl), 1)
    for g in range(GMLP_GROUPS):
        wg = jnp.where(r_i >= c_i, ws_ref[g], 0.0).astype(BF16)
        bg = bst_ref[:, g:g + 1]
        gs = slice(g * GMLP_GROUP_DIM, (g + 1) * GMLP_GROUP_DIM)
        for c in range(tm // cl):
            rs = slice(c * cl, (c + 1) * cl)
            mix = jnp.dot(wg, vnb[rs, gs], preferred_element_type=F32) + bg
            ob_ref[rs, gs] = (u[rs, gs] * mix).astype(BF16)


def _inproj(x2d, w_a, tabs, kig, kib, sg, sb, ws, bst, *, tm, cl, emit_vn, v_transposed):
    m = x2d.shape[0]
    cos, sn, sp = tabs
    nt = cos.shape[0] // tm
    row = lambda w: pl.BlockSpec((tm, w), lambda i: (i, 0))
    tab = pl.BlockSpec((tm, LANES), lambda i: (i % nt, 0))
    full = lambda a: pl.BlockSpec(a.shape, lambda i: (0,) * a.ndim)
    if v_transposed:
        vb_shape = jax.ShapeDtypeStruct((m // tm, N_KV_HEADS * VT_ROWS, tm), BF16)
        vb_spec = pl.BlockSpec((None, N_KV_HEADS * VT_ROWS, tm), lambda i: (i, 0, 0))
    else:
        vb_shape, vb_spec = jax.ShapeDtypeStruct((m, KV_DIM), BF16), row(KV_DIM)
    out_shapes = [
        jax.ShapeDtypeStruct((m, N_HEADS * LANES), BF16),
        jax.ShapeDtypeStruct((m, KV_DIM), F32),
        jax.ShapeDtypeStruct((m, KV_DIM), F32),
        jax.ShapeDtypeStruct((m, KV_DIM), BF16),
        vb_shape,
        jax.ShapeDtypeStruct((m, IDX_HEADS * 4 * IDX_DIM), BF16),
        jax.ShapeDtypeStruct((m, LANES), F32),
        jax.ShapeDtypeStruct((m, 4 * IDX_DIM), BF16),
        jax.ShapeDtypeStruct((m, GMLP_DIM), BF16),
    ]
    out_specs = [row(N_HEADS * LANES), row(KV_DIM), row(KV_DIM), row(KV_DIM), vb_spec,
                 row(IDX_HEADS * 4 * IDX_DIM), row(LANES), row(4 * IDX_DIM), row(GMLP_DIM)]
    if emit_vn:
        out_shapes.append(jax.ShapeDtypeStruct((m, GMLP_DIM), F32))
        out_specs.append(row(GMLP_DIM))
    return pl.pallas_call(
        functools.partial(_inproj_body, cl=cl, v_transposed=v_transposed),
        grid=(m // tm,),
        in_specs=[row(D_MODEL), full(w_a), tab, tab, tab, full(kig), full(kib), full(sg), full(sb),
                  full(ws), full(bst)],
        out_specs=out_specs,
        out_shape=out_shapes,
        compiler_params=_cparams(("parallel",)),
        name="inproj",
    )(x2d, w_a, cos, sn, sp, kig, kib, sg, sb, ws, bst)


def _nt_dot(a, b):
    return lax.dot_general(a, b, (((1,), (1,)), ((), ())), preferred_element_type=F32)


def _tree_sum(parts):
    while len(parts) > 1:
        parts = [parts[j] + parts[j + 1] for j in range(0, len(parts) - 1, 2)] + (
            [parts[-1]] if len(parts) % 2 else [])
    return parts[0]


def _attn_body(wit_ref, qp_ref, qia_ref, kihl_ref, kb_ref, vt_ref, o_ref,
               keys_ref, hi_ref, bias_ref, s_ref, *, qb, pos0, s_true, topk):
    i = pl.program_id(1)
    sc_n = SEARCH_CHUNK
    cpt = KEY_TILE // sc_n
    grp = N_HEADS // N_KV_HEADS
    qlane = lax.broadcasted_iota(jnp.int32, (1, qb), 1)
    qpos = pos0 + i * qb + qlane
    limit = jnp.minimum((qpos // CHUNK + 1) * CHUNK, s_true)
    kv_len = jnp.minimum(((pos0 + i * qb + qb - 1) // CHUNK + 1) * CHUNK, s_true)
    n_kt = (kv_len + KEY_TILE - 1) // KEY_TILE
    n_c = (kv_len + sc_n - 1) // sc_n
    krow = lax.broadcasted_iota(jnp.int32, (sc_n, 1), 0)

    wit = wit_ref[...] * WI_SCALE
    hw = 4 * IDX_DIM
    pairs = [jnp.concatenate([qia_ref[:, (2 * hp) * hw:(2 * hp + 1) * hw],
                              qia_ref[:, (2 * hp + 1) * hw:(2 * hp + 2) * hw]], axis=0)
             for hp in range(IDX_HEADS // 2)]

    def score_tiles(tiles):
        for t in tiles:
            kt = kihl_ref[pl.ds(pl.multiple_of(t * KEY_TILE, KEY_TILE), KEY_TILE), :]
            sc = jnp.zeros((KEY_TILE, qb), F32)
            for hp, a2 in enumerate(pairs):
                dots = jnp.maximum(_nt_dot(kt, a2), 0.0)
                sc = (sc + wit[2 * hp:2 * hp + 1, :] * dots[:, :qb]
                      + wit[2 * hp + 1:2 * hp + 2, :] * dots[:, qb:])
            for j in range(cpt):
                kidx = t * KEY_TILE + j * sc_n + krow
                s_j = jnp.where(kidx < limit, sc[j * sc_n:(j + 1) * sc_n], NEG_INF)
                s_j = jnp.where(s_j == 0.0, 0.0, s_j)
                bits = lax.bitcast_convert_type(s_j, jnp.int32)
                keys_ref[t * cpt + j] = bits ^ ((bits >> 31) & 0x7FFFFFFF)
                hi_ref[t * cpt + j] = lax.bitcast_convert_type(bits & jnp.int32(-65536), F32).astype(BF16)

    n_pair = n_kt // 2
    odd = n_kt % 2 == 1

    def score_pair(pr, carry):
        score_tiles((2 * pr, 2 * pr + 1))
        return carry

    lax.fori_loop(0, n_pair, score_pair, 0)
    pl.when(odd)(lambda: score_tiles((n_kt - 1,)))

    n_acc = 4

    def count(ref, pred, thr, pack):
        rows_per = 8 * pack
        def body(c, accs):
            vals = ref[c]
            ones = jnp.where(pred(vals, thr), jnp.ones((), vals.dtype if pack == 2 else F32),
                             jnp.zeros((), vals.dtype if pack == 2 else F32))
            rows = [ones[rows_per * j:rows_per * (j + 1)] for j in range(sc_n // rows_per)]
            per = len(rows) // n_acc
            return tuple(a + _tree_sum(rows[k * per:(k + 1) * per]) for k, a in enumerate(accs))
        acc_dtype = BF16 if pack == 2 else F32
        accs = lax.fori_loop(0, n_c, body, tuple(jnp.zeros((rows_per, qb), acc_dtype) for _ in range(n_acc)))
        return jnp.sum(_tree_sum([a.astype(F32) for a in accs]), axis=0, keepdims=True)

    ge = lambda vals, t_: vals >= t_
    key16_ninf, key16_pinf = -32641, 32640

    def half_key_to_bf16(k16):
        c = jnp.clip(k16, key16_ninf, key16_pinf)
        c = jnp.where((c >= 1) & (c < 128), 128, c)
        c = jnp.where((c <= -2) & (c >= -128), -1, c)
        b16 = jnp.where(c < 0, c ^ 0x7FFF, c)
        return lax.bitcast_convert_type(jnp.left_shift(b16, 16), F32).astype(BF16)

    def top_step(it, carry):
        thr, best = carry
        cand = thr + jnp.left_shift(jnp.int32(1), 15 - it)
        cnt = count(hi_ref, ge, half_key_to_bf16(cand), 2)
        ok = cnt >= topk
        return jnp.where(ok, cand, thr), jnp.where(ok, cnt, best)

    thr16, n_ge = lax.fori_loop(0, 16, top_step,
                                (jnp.full((1, qb), -2 ** 15, jnp.int32), jnp.zeros((1, qb), F32)))

    def low_step(it, carry):
        thr, best = carry
        cand = thr + jnp.left_shift(jnp.int32(1), 15 - it)
        cnt = count(keys_ref, ge, cand, 1)
        ok = cnt >= topk
        return jnp.where(ok, cand, thr), jnp.where(ok, cnt, best)

    thr, n_ge = lax.fori_loop(0, 16, low_step, (jnp.minimum(thr16, key16_pinf) * 65536, n_ge))

    overflow = jnp.max(n_ge) > topk

    @pl.when(jnp.logical_not(overflow))
    def _():
        def body(c, carry):
            sel = (keys_ref[c] >= thr) & (c * sc_n + krow < limit)
            bias_ref[c] = jnp.where(sel, 0.0, NEG_INF)
            return carry
        lax.fori_loop(0, n_kt * cpt, body, 0)

    @pl.when(overflow)
    def _():
        need = topk - count(keys_ref, lambda kk, t_: kk > t_, thr, 1)
        r_i = lax.broadcasted_iota(jnp.int32, (sc_n, sc_n), 0)
        c_i = lax.broadcasted_iota(jnp.int32, (sc_n, sc_n), 1)
        lower = jnp.where(c_i <= r_i, 1.0, 0.0).astype(BF16)

        def tie_chunk(c, off):
            kk = keys_ref[c]
            eq = kk == thr
            local = jnp.dot(lower, jnp.where(eq, 1.0, 0.0).astype(BF16), preferred_element_type=F32)
            take = eq & (local + off <= need)
            sel = ((kk > thr) | take) & (c * sc_n + krow < limit)
            bias_ref[c] = jnp.where(sel, 0.0, NEG_INF)
            return off + local[sc_n - 1:sc_n, :]

        lax.fori_loop(0, n_kt * cpt, tie_chunk, jnp.zeros((1, qb), F32))

    r_e = lax.broadcasted_iota(jnp.int32, (qb, qb), 0)
    c_e = lax.broadcasted_iota(jnp.int32, (qb, qb), 1)
    eye = jnp.where(r_e == c_e, 1.0, 0.0).astype(BF16)
    kv_heads = range(N_KV_HEADS)
    qns = [jnp.concatenate(
        [qp_ref[:, (grp * n + g) * LANES:(grp * n + g + 1) * LANES] for g in range(grp)], axis=0)
        for n in kv_heads]

    def pass_a(tiles, ms):
        ms = list(ms)
        for t in tiles:
            kt = kb_ref[pl.ds(pl.multiple_of(t * KEY_TILE, KEY_TILE), KEY_TILE), :]
            bias = jnp.concatenate([bias_ref[t * cpt + j] for j in range(cpt)], axis=0)
            bias = jnp.concatenate([bias] * grp, axis=1)
            for n in kv_heads:
                s = _nt_dot(kt, qns[n]) + bias
                s_ref[n, t] = s
                ms[n] = jnp.maximum(ms[n], jnp.max(s, axis=0, keepdims=True))
        return tuple(ms)

    def pass_b(tiles, accs, ms):
        accs = list(accs)
        for t in tiles:
            for n in kv_heads:
                p = jnp.exp(s_ref[n, t] - ms[n]).astype(BF16)
                vt = vt_ref[t, n * VT_ROWS:(n + 1) * VT_ROWS, :]
                accs[n] = accs[n] + jnp.dot(vt, p, preferred_element_type=F32)
        return tuple(accs)

    ms = tuple(jnp.full((1, grp * qb), M_INIT, F32) for _ in kv_heads)
    ms = lax.fori_loop(0, n_pair, lambda pr, c: pass_a((2 * pr, 2 * pr + 1), c), ms)
    ms = lax.cond(odd, lambda c: pass_a((n_kt - 1,), c), lambda c: c, ms)
    accs = tuple(jnp.zeros((VT_ROWS, grp * qb), F32) for _ in kv_heads)
    accs = lax.fori_loop(0, n_pair, lambda pr, c: pass_b((2 * pr, 2 * pr + 1), c, ms), accs)
    accs = lax.cond(odd, lambda c: pass_b((n_kt - 1,), c, ms), lambda c: c, accs)

    for n in kv_heads:
        acc = accs[n]
        ot = (acc[:HEAD_DIM] / acc[HEAD_DIM:HEAD_DIM + 1]).astype(BF16)
        for g in range(grp):
            h = grp * n + g
            o_ref[:, h * HEAD_DIM:(h + 1) * HEAD_DIM] = _nt_dot(
                eye, ot[:, g * qb:(g + 1) * qb]).astype(o_ref.dtype)


def _attention(wit, qp, qia, kihl, kb, vt, *, qb, pos0, s_true, topk):
    b, t, _ = qp.shape
    s_pad = kihl.shape[1]
    assert t % qb == 0 and s_pad % KEY_TILE == 0
    grp = N_HEADS // N_KV_HEADS
    qblk = lambda w: pl.BlockSpec((None, qb, w), lambda bi, i: (bi, i, 0))
    kblk = lambda w: pl.BlockSpec((None, s_pad, w), lambda bi, i: (bi, 0, 0))
    return pl.pallas_call(
        functools.partial(_attn_body, qb=qb, pos0=pos0, s_true=s_true, topk=topk),
        grid=(b, t // qb),
        in_specs=[pl.BlockSpec((None, None, 8, qb), lambda bi, i: (bi, i, 0, 0)),
                  qblk(N_HEADS * LANES), qblk(IDX_HEADS * 4 * IDX_DIM),
                  kblk(4 * IDX_DIM), kblk(KV_DIM),
                  pl.BlockSpec((None, s_pad // KEY_TILE, N_KV_HEADS * VT_ROWS, KEY_TILE),
                               lambda bi, i: (bi, 0, 0, 0))],
        out_specs=qblk(ATTN_DIM),
        out_shape=jax.ShapeDtypeStruct((b, t, ATTN_DIM), BF16),
        scratch_shapes=[
            pltpu.VMEM((s_pad // SEARCH_CHUNK, SEARCH_CHUNK, qb), jnp.int32),
            pltpu.VMEM((s_pad // SEARCH_CHUNK, SEARCH_CHUNK, qb), BF16),
            pltpu.VMEM((s_pad // SEARCH_CHUNK, SEARCH_CHUNK, qb), F32),
            pltpu.VMEM((N_KV_HEADS, s_pad // KEY_TILE, KEY_TILE, grp * qb), F32),
        ],
        compiler_params=_cparams(("parallel", "arbitrary")),
        name="sparse_attn",
    )(wit, qp, qia, kihl, kb, vt)


def _merge_body(x_ref, oa_ref, ob_ref, wga_ref, wgb_ref, wpa_ref, wpb_ref, wout_ref, g_ref, b_ref, o_ref):
    x = x_ref[...]
    xb = x.astype(BF16)
    ga = jnp.dot(xb, wga_ref[...], preferred_element_type=F32)
    gb = jnp.dot(xb, wgb_ref[...], preferred_element_type=F32)
    pa = jnp.dot(oa_ref[...], wpa_ref[...], preferred_element_type=F32)
    pb = jnp.dot(ob_ref[...], wpb_ref[...], preferred_element_type=F32)
    merged = jax.nn.sigmoid(ga) * pa + jax.nn.sigmoid(gb) * pb
    h = ALPHA * x + jnp.dot(merged.astype(BF16), wout_ref[...], preferred_element_type=F32)
    o_ref[...] = _layer_norm(h, g_ref[...], b_ref[...])


def _merge(x2d, oa, ob, wga, wgb, wpa, wpb, wout, g, b, *, tm):
    m = x2d.shape[0]
    row = lambda w: pl.BlockSpec((tm, w), lambda i: (i, 0))
    full = lambda a: pl.BlockSpec(a.shape, lambda i: (0,) * a.ndim)
    return pl.pallas_call(
        _merge_body,
        grid=(m // tm,),
        in_specs=[row(D_MODEL), row(ATTN_DIM), row(GMLP_DIM), full(wga), full(wgb), full(wpa), full(wpb),
                  full(wout), full(g), full(b)],
        out_specs=row(D_MODEL),
        out_shape=jax.ShapeDtypeStruct((m, D_MODEL), F32),
        compiler_params=_cparams(("parallel",)),
        name="merge",
    )(x2d, oa, ob, wga, wgb, wpa, wpb, wout, g, b)


FF_SPLIT = 4


def _ffn_body(x_ref, w1_ref, w2_ref, g_ref, b_ref, o_ref):
    x = x_ref[...]
    xb = x.astype(BF16)
    cw = D_FF // FF_SPLIT
    ff = jnp.zeros(x.shape, F32)
    for c in range(FF_SPLIT):
        hcol = jnp.maximum(jnp.dot(xb, w1_ref[:, c * cw:(c + 1) * cw], preferred_element_type=F32), 0.0)
        ff = ff + jnp.dot((hcol * hcol).astype(BF16), w2_ref[c * cw:(c + 1) * cw, :],
                          preferred_element_type=F32)
    o_ref[...] = _layer_norm(ALPHA * x + ff, g_ref[...], b_ref[...])


def _ffn(x2d, w1, w2, g, b, *, tm):
    m = x2d.shape[0]
    row = lambda w: pl.BlockSpec((tm, w), lambda i: (i, 0))
    full = lambda a: pl.BlockSpec(a.shape, lambda i: (0,) * a.ndim)
    return pl.pallas_call(
        _ffn_body,
        grid=(m // tm,),
        in_specs=[row(D_MODEL), full(w1), full(w2), full(g), full(b)],
        out_specs=row(D_MODEL),
        out_shape=jax.ShapeDtypeStruct((m, D_MODEL), F32),
        compiler_params=_cparams(("parallel",)),
        name="ffn",
    )(x2d, w1, w2, g, b)


def _rope_tables(pos, rows):
    half = ROPE_DIM // 2
    freqs = ROPE_THETA ** (-jnp.arange(half, dtype=F32) * 2.0 / ROPE_DIM)
    ang = pos.astype(F32)[:, None] * freqs[None, :]
    cos, sin = jnp.cos(ang), jnp.sin(ang)
    t = pos.shape[0]
    rest = HEAD_DIM - ROPE_DIM
    cos_h = jnp.concatenate([cos, cos, jnp.ones((t, rest), F32)], axis=1)
    sn_h = jnp.concatenate([-sin, jnp.zeros((t, half + rest), F32)], axis=1)
    sp_h = jnp.concatenate([jnp.zeros((t, half), F32), sin, jnp.zeros((t, rest), F32)], axis=1)
    reps = (max(rows // t, 1), LANES // HEAD_DIM)
    return tuple(jnp.tile(a, reps) for a in (cos_h, sn_h, sp_h))


def _split_w_in(w):
    wb = w.astype(BF16)
    n_head = OFF_KIW + IDX_DIM + IDX_HEADS
    col = lax.broadcasted_iota(jnp.int32, (1, OFF_U), 1)
    head = jnp.where(col < n_head, wb[:, :OFF_U], jnp.zeros((), BF16))
    tail = wb[:, n_head:]
    w_a = jnp.concatenate([head, tail[:, :2 * GMLP_DIM]], axis=1)
    return w_a, tail[:, 2 * GMLP_DIM:2 * GMLP_DIM + D_MODEL], tail[:, 2 * GMLP_DIM + D_MODEL:]


def _split_hi_lo(ki):
    hi = ki.astype(BF16)
    lo = (ki - hi.astype(F32)).astype(BF16)
    return jnp.concatenate([hi, hi, lo, lo], axis=-1)


def _head_weights_t(kiw):
    b, t, _ = kiw.shape
    wi = kiw[:, :, IDX_DIM:IDX_DIM + IDX_HEADS].reshape(b, t // Q_BLOCK, Q_BLOCK, IDX_HEADS)
    return jnp.pad(wi.transpose(0, 1, 3, 2), ((0, 0), (0, 0), (0, 8 - IDX_HEADS), (0, 0)))


def _augment_vt(vt):
    b, nt, _, kt = vt.shape
    v4 = vt.reshape(b, nt, N_KV_HEADS, HEAD_DIM, kt)
    ones = jnp.ones((b, nt, N_KV_HEADS, 1, kt), vt.dtype)
    zeros = jnp.zeros((b, nt, N_KV_HEADS, VT_ROWS - HEAD_DIM - 1, kt), vt.dtype)
    return jnp.concatenate([v4, ones, zeros], axis=3).reshape(b, nt, N_KV_HEADS * VT_ROWS, kt)


def _pad_keys(a, s_pad):
    return jnp.pad(a, ((0, 0), (0, s_pad - a.shape[1]), (0, 0)))


def kernel(x_prompt, x_sample, cache_k, cache_v, cache_idx_k, w_in, idx_k_g, idx_k_b, sgu_ln_g, sgu_ln_b,
           w_s, b_s, w_pa, w_pb, w_out, ln1_g, ln1_b, w_ff1, w_ff2, ln2_g, ln2_b):
    bp, tp, _ = x_prompt.shape
    bs, ts, _ = x_sample.shape
    depth = w_in.shape[0]
    past = cache_k.shape[2]
    topk_p = min(TOPK_MAX, tp // 4)
    topk_s = min(TOPK_MAX, (past + ts) // 4)
    tm_p = min(512, bp * tp)
    tm_s = bs * ts
    s_all = past + ts
    s_pad = -(-s_all // KEY_TILE) * KEY_TILE

    tabs_p = _rope_tables(jnp.arange(tp), tm_p)
    tabs_s = _rope_tables(past + jnp.arange(ts), tm_s)

    xp = x_prompt.reshape(bp * tp, D_MODEL)
    xs = x_sample.reshape(bs * ts, D_MODEL)
    pk, pv, pik, sk, sv, sik, ssv = [], [], [], [], [], [], []
    row2 = lambda a: a.reshape(1, -1)
    for l in range(depth):
        w_a, w_ga, w_gb = _split_w_in(w_in[l])
        kig = row2(jnp.pad(idx_k_g[l], (0, LANES - IDX_DIM)))
        kib = row2(jnp.pad(idx_k_b[l], (0, LANES - IDX_DIM)))
        sg, sb = row2(sgu_ln_g[l]), row2(sgu_ln_b[l])
        wpa, wpb, wout = w_pa[l].astype(BF16), w_pb[l].astype(BF16), w_out[l].astype(BF16)
        w1, w2 = w_ff1[l].astype(BF16), w_ff2[l].astype(BF16)
        g1, b1, g2, b2 = row2(ln1_g[l]), row2(ln1_b[l]), row2(ln2_g[l]), row2(ln2_b[l])

        qp, kf, vf, kb, vt, qia, kiw, kihl, ob = _inproj(
            xp, w_a, tabs_p, kig, kib, sg, sb, w_s[l], b_s[l].T, tm=tm_p, cl=GMLP_CHUNK, emit_vn=False,
            v_transposed=True)
        r3 = lambda a: a.reshape(bp, tp, a.shape[-1])
        oa = _attention(_head_weights_t(r3(kiw)), r3(qp), r3(qia), r3(kihl), r3(kb),
                        vt.reshape(bp, tp // KEY_TILE, N_KV_HEADS * VT_ROWS, KEY_TILE),
                        qb=Q_BLOCK, pos0=0, s_true=tp, topk=topk_p)
        x1 = _merge(xp, oa.reshape(bp * tp, ATTN_DIM), ob, w_ga, w_gb, wpa, wpb, wout, g1, b1, tm=tm_p)
        xp = _ffn(x1, w1, w2, g2, b2, tm=tm_p)
        pk.append(kf.reshape(bp, tp, N_KV_HEADS, HEAD_DIM))
        pv.append(vf.reshape(bp, tp, N_KV_HEADS, HEAD_DIM))
        pik.append(kiw[:, :IDX_DIM].reshape(bp, tp, IDX_DIM))

        qp, kf, vf, kb, vb, qia, kiw, kihl, ob, vn = _inproj(
            xs, w_a, tabs_s, kig, kib, sg, sb, w_s[l][:, :ts, :ts], b_s[l][:, :ts].T,
            tm=tm_s, cl=ts, emit_vn=True, v_transposed=False)
        r3 = lambda a: a.reshape(bs, ts, a.shape[-1])
        k_all = _pad_keys(jnp.concatenate([cache_k[l].reshape(bs, past, KV_DIM).astype(BF16), r3(kb)], 1), s_pad)
        v_all = _pad_keys(jnp.concatenate([cache_v[l].reshape(bs, past, KV_DIM).astype(BF16), r3(vb)], 1), s_pad)
        vt_all = _augment_vt(v_all.reshape(bs, s_pad // KEY_TILE, KEY_TILE, KV_DIM).transpose(0, 1, 3, 2))
        ki_all = _pad_keys(jnp.concatenate([_split_hi_lo(cache_idx_k[l]), r3(kihl)], 1), s_pad)
        padq = lambda a: jnp.pad(r3(a), ((0, 0), (0, Q_BLOCK - ts), (0, 0)))
        oa = _attention(_head_weights_t(padq(kiw)), padq(qp), padq(qia), ki_all, k_all, vt_all,
                        qb=Q_BLOCK, pos0=past, s_true=s_all, topk=topk_s)[:, :ts]
        x1 = _merge(xs, oa.reshape(bs * ts, ATTN_DIM), ob, w_ga, w_gb, wpa, wpb, wout, g1, b1, tm=tm_s)
        xs = _ffn(x1, w1, w2, g2, b2, tm=tm_s)
        sk.append(kf.reshape(bs, ts, N_KV_HEADS, HEAD_DIM))
        sv.append(vf.reshape(bs, ts, N_KV_HEADS, HEAD_DIM))
        sik.append(kiw[:, :IDX_DIM].reshape(bs, ts, IDX_DIM))
        ssv.append(vn.reshape(bs, ts, GMLP_DIM))

    return (xp.reshape(bp, tp, D_MODEL), xs.reshape(bs, ts, D_MODEL),
            jnp.stack(pk), jnp.stack(pv), jnp.stack(pik),
            jnp.stack(sk), jnp.stack(sv), jnp.stack(sik), jnp.stack(ssv))
```

```python
import functools

import jax
import jax.numpy as jnp
from jax import lax
from jax.experimental import pallas as pl
from jax.experimental.pallas import tpu as pltpu

F32 = jnp.float32
BF16 = jnp.bfloat16

D_MODEL = 1024
N_HEADS = 8
N_KV_HEADS = 2
HEAD_DIM = 64
ATTN_DIM = N_HEADS * HEAD_DIM
KV_DIM = N_KV_HEADS * HEAD_DIM
IDX_HEADS = 4
IDX_DIM = 64
TOPK_MAX = 256
CHUNK = 64
GMLP_CHUNK = 128
GMLP_GROUPS = 4
GMLP_DIM = 512
GMLP_GROUP_DIM = GMLP_DIM // GMLP_GROUPS
D_FF = 4 * D_MODEL
ROPE_THETA = 500000.0
ROPE_DIM = HEAD_DIM // 4
DEPTH = 2
ALPHA = (2 * DEPTH) ** 0.25
LN_EPS = 1e-5

LANES = 128
Q_SCALE = HEAD_DIM ** -0.5
IDX_SCALE = IDX_DIM ** -0.5
WI_SCALE = IDX_HEADS ** -0.5
NEG_INF = float("-inf")
M_INIT = -1e30

OFF_Q, OFF_K, OFF_V, OFF_QI, OFF_KIW, OFF_U, OFF_VG = 0, 512, 640, 768, 1024, 1152, 1664
IN_COLS = 2176
KEY_TILE = 512
Q_BLOCK = LANES
SEARCH_CHUNK = 512
VT_ROWS = 80
VMEM_LIMIT = 56 * 1024 * 1024


def _cparams(sem):
    return pltpu.CompilerParams(dimension_semantics=sem, vmem_limit_bytes=VMEM_LIMIT)


def _layer_norm(x, g, b):
    mu = jnp.mean(x, axis=-1, keepdims=True)
    d = x - mu
    var = jnp.mean(d * d, axis=-1, keepdims=True)
    return d * lax.rsqrt(var + LN_EPS) * g + b


def _rope(y, cos, sn, sp):
    w = y.shape[-1]
    half = ROPE_DIM // 2
    return y * cos + pltpu.roll(y, w - half, 1) * sn + pltpu.roll(y, half, 1) * sp


def _tile_lanes(a, n):
    return a if n == 1 else jnp.concatenate([a] * n, axis=1)


def _inproj_body(x_ref, w_ref, cos_ref, sn_ref, sp_ref, kig_ref, kib_ref, sg_ref, sb_ref,
                 ws_ref, bst_ref,
                 qp_ref, kf_ref, vf_ref, kb_ref, vb_ref, qia_ref, kiw_ref, kihl_ref, ob_ref,
                 *vn_refs, cl, v_transposed):
    tm = x_ref.shape[0]
    xb = x_ref[...].astype(BF16)

    def proj(lo, n):
        return jnp.dot(xb, w_ref[:, lo:lo + n], preferred_element_type=F32)

    cos, sn, sp = cos_ref[...], sn_ref[...], sp_ref[...]
    lane = lax.broadcasted_iota(jnp.int32, (tm, LANES), 1)
    left = lane < IDX_DIM

    q = _rope(proj(OFF_Q, ATTN_DIM), _tile_lanes(cos, 4), _tile_lanes(sn, 4), _tile_lanes(sp, 4)) * Q_SCALE
    heads_per_kv = N_HEADS // N_KV_HEADS
    for j in range(ATTN_DIM // LANES):
        tile = q[:, j * LANES:(j + 1) * LANES]
        rolled = pltpu.roll(tile, HEAD_DIM, 1)
        on_left = (2 * j) // heads_per_kv == 0
        keep = left if on_left else jnp.logical_not(left)
        even = jnp.where(keep, tile if on_left else rolled, 0.0)
        odd = jnp.where(keep, rolled if on_left else tile, 0.0)
        qp_ref[:, (2 * j) * LANES:(2 * j + 1) * LANES] = even.astype(BF16)
        qp_ref[:, (2 * j + 1) * LANES:(2 * j + 2) * LANES] = odd.astype(BF16)

    k = _rope(proj(OFF_K, KV_DIM), cos, sn, sp)
    kf_ref[...] = k
    kb_ref[...] = k.astype(BF16)
    v = proj(OFF_V, KV_DIM)
    vf_ref[...] = v
    vb = v.astype(BF16)
    if v_transposed:
        r_e = lax.broadcasted_iota(jnp.int32, (N_KV_HEADS * VT_ROWS, LANES), 0)
        c_e = lax.broadcasted_iota(jnp.int32, (N_KV_HEADS * VT_ROWS, LANES), 1)
        head, d = r_e // VT_ROWS, r_e % VT_ROWS
        sel = jnp.where((d < HEAD_DIM) & (c_e == head * HEAD_DIM + d), 1.0, 0.0).astype(BF16)
        vt = lax.dot_general(sel, vb, (((1,), (1,)), ((), ())), preferred_element_type=F32)
        r_o = lax.broadcasted_iota(jnp.int32, (N_KV_HEADS * VT_ROWS, tm), 0)
        vb_ref[...] = jnp.where(r_o % VT_ROWS == HEAD_DIM, 1.0, vt).astype(BF16)
    else:
        vb_ref[...] = vb

    qi = _rope(proj(OFF_QI, IDX_HEADS * IDX_DIM), _tile_lanes(cos, 2), _tile_lanes(sn, 2),
               _tile_lanes(sp, 2)) * IDX_SCALE
    qi_hi = qi.astype(BF16).astype(F32)
    qi_lo = qi - qi_hi
    for t in range(IDX_HEADS // 2):
        h_t = qi_hi[:, t * LANES:(t + 1) * LANES]
        l_t = qi_lo[:, t * LANES:(t + 1) * LANES]
        a0 = jnp.where(left, h_t, pltpu.roll(l_t, IDX_DIM, 1))
        a1 = jnp.where(left, pltpu.roll(h_t, IDX_DIM, 1), l_t)
        for j, a in enumerate((a0, a1)):
            base = (2 * t + j) * 2 * LANES
            qia_ref[:, base:base + 2 * LANES] = jnp.concatenate([a, a], axis=1).astype(BF16)

    slab = proj(OFF_KIW, LANES)
    mu = jnp.sum(jnp.where(left, slab, 0.0), axis=1, keepdims=True) * (1.0 / IDX_DIM)
    d = jnp.where(left, slab - mu, 0.0)
    var = jnp.sum(d * d, axis=1, keepdims=True) * (1.0 / IDX_DIM)
    kin = d * lax.rsqrt(var + LN_EPS) * kig_ref[...] + kib_ref[...]
    slab = jnp.where(left, kin, slab)
    slab = _rope(slab, jnp.where(left, cos, 1.0), jnp.where(left, sn, 0.0), jnp.where(left, sp, 0.0))
    kiw_ref[...] = slab
    kz = jnp.where(left, slab, 0.0)
    k_hi = kz.astype(BF16).astype(F32)
    k_lo = kz - k_hi
    hh = k_hi + pltpu.roll(k_hi, IDX_DIM, 1)
    ll = k_lo + pltpu.roll(k_lo, IDX_DIM, 1)
    kihl_ref[...] = jnp.concatenate([hh, ll], axis=1).astype(BF16)

    u = proj(OFF_U, GMLP_DIM)
    vn = _layer_norm(proj(OFF_VG, GMLP_DIM), sg_ref[...], sb_ref[...])
    if vn_refs:
        vn_refs[0][...] = vn
    vnb = vn.astype(BF16)
    r_i = lax.broadcasted_iota(jnp.int32, (cl, cl), 0)
    c_i = lax.broadcasted_iota(jnp.int32, (cl, cl), 1)
    for g in range(GMLP_GROUPS):
        wg = jnp.where(r_i >= c_i, ws_ref[g], 0.0).astype(BF16)
        bg = bst_ref[:, g:g + 1]
        gs = slice(g * GMLP_GROUP_DIM, (g + 1) * GMLP_GROUP_DIM)
        for c in range(tm // cl):
            rs = slice(c * cl, (c + 1) * cl)
            mix = jnp.dot(wg, vnb[rs, gs], preferred_element_type=F32) + bg
            ob_ref[rs, gs] = (u[rs, gs] * mix).astype(BF16)


def _inproj(x2d, w_a, tabs, kig, kib, sg, sb, ws, bst, *, tm, cl, emit_vn, v_transposed):
    m = x2d.shape[0]
    cos, sn, sp = tabs
    nt = cos.shape[0] // tm
    row = lambda w: pl.BlockSpec((tm, w), lambda i: (i, 0))
    tab = pl.BlockSpec((tm, LANES), lambda i: (i % nt, 0))
    full = lambda a: pl.BlockSpec(a.shape, lambda i: (0,) * a.ndim)
    if v_transposed:
        vb_shape = jax.ShapeDtypeStruct((m // tm, N_KV_HEADS * VT_ROWS, tm), BF16)
        vb_spec = pl.BlockSpec((None, N_KV_HEADS * VT_ROWS, tm), lambda i: (i, 0, 0))
    else:
        vb_shape, vb_spec = jax.ShapeDtypeStruct((m, KV_DIM), BF16), row(KV_DIM)
    out_shapes = [
        jax.ShapeDtypeStruct((m, N_HEADS * LANES), BF16),
        jax.ShapeDtypeStruct((m, KV_DIM), F32),
        jax.ShapeDtypeStruct((m, KV_DIM), F32),
        jax.ShapeDtypeStruct((m, KV_DIM), BF16),
        vb_shape,
        jax.ShapeDtypeStruct((m, IDX_HEADS * 4 * IDX_DIM), BF16),
        jax.ShapeDtypeStruct((m, LANES), F32),
        jax.ShapeDtypeStruct((m, 4 * IDX_DIM), BF16),
        jax.ShapeDtypeStruct((m, GMLP_DIM), BF16),
    ]
    out_specs = [row(N_HEADS * LANES), row(KV_DIM), row(KV_DIM), row(KV_DIM), vb_spec,
                 row(IDX_HEADS * 4 * IDX_DIM), row(LANES), row(4 * IDX_DIM), row(GMLP_DIM)]
    if emit_vn:
        out_shapes.append(jax.ShapeDtypeStruct((m, GMLP_DIM), F32))
        out_specs.append(row(GMLP_DIM))
    return pl.pallas_call(
        functools.partial(_inproj_body, cl=cl, v_transposed=v_transposed),
        grid=(m // tm,),
        in_specs=[row(D_MODEL), full(w_a), tab, tab, tab, full(kig), full(kib), full(sg), full(sb),
                  full(ws), full(bst)],
        out_specs=out_specs,
        out_shape=out_shapes,
        compiler_params=_cparams(("parallel",)),
        name="inproj",
    )(x2d, w_a, cos, sn, sp, kig, kib, sg, sb, ws, bst)


def _nt_dot(a, b):
    return lax.dot_general(a, b, (((1,), (1,)), ((), ())), preferred_element_type=F32)


def _tree_sum(parts):
    while len(parts) > 1:
        parts = [parts[j] + parts[j + 1] for j in range(0, len(parts) - 1, 2)] + (
            [parts[-1]] if len(parts) % 2 else [])
    return parts[0]


def _attn_body(wit_ref, qp_ref, qia_ref, kihl_ref, kb_ref, vt_ref, o_ref,
               keys_ref, bias_ref, s_ref, *, qb, pos0, s_true, topk):
    i = pl.program_id(1)
    sc_n = SEARCH_CHUNK
    cpt = KEY_TILE // sc_n
    grp = N_HEADS // N_KV_HEADS
    qlane = lax.broadcasted_iota(jnp.int32, (1, qb), 1)
    qpos = pos0 + i * qb + qlane
    limit = jnp.minimum((qpos // CHUNK + 1) * CHUNK, s_true)
    kv_len = jnp.minimum(((pos0 + i * qb + qb - 1) // CHUNK + 1) * CHUNK, s_true)
    n_kt = (kv_len + KEY_TILE - 1) // KEY_TILE
    n_c = (kv_len + sc_n - 1) // sc_n
    krow = lax.broadcasted_iota(jnp.int32, (sc_n, 1), 0)

    wit = wit_ref[...] * WI_SCALE
    hw = 4 * IDX_DIM
    pairs = [jnp.concatenate([qia_ref[:, (2 * hp) * hw:(2 * hp + 1) * hw],
                              qia_ref[:, (2 * hp + 1) * hw:(2 * hp + 2) * hw]], axis=0)
             for hp in range(IDX_HEADS // 2)]

    def score_tiles(tiles):
        for t in tiles:
            kt = kihl_ref[pl.ds(pl.multiple_of(t * KEY_TILE, KEY_TILE), KEY_TILE), :]
            sc = jnp.zeros((KEY_TILE, qb), F32)
            for hp, a2 in enumerate(pairs):
                dots = jnp.maximum(_nt_dot(kt, a2), 0.0)
                sc = (sc + wit[2 * hp:2 * hp + 1, :] * dots[:, :qb]
                      + wit[2 * hp + 1:2 * hp + 2, :] * dots[:, qb:])
            for j in range(cpt):
                kidx = t * KEY_TILE + j * sc_n + krow
                s_j = jnp.where(kidx < limit, sc[j * sc_n:(j + 1) * sc_n], NEG_INF)
                s_j = jnp.where(s_j == 0.0, 0.0, s_j)
                bits = lax.bitcast_convert_type(s_j, jnp.int32)
                keys_ref[t * cpt + j] = bits ^ ((bits >> 31) & 0x7FFFFFFF)

    n_pair = n_kt // 2
    odd = n_kt % 2 == 1

    def score_pair(pr, carry):
        score_tiles((2 * pr, 2 * pr + 1))
        return carry

    lax.fori_loop(0, n_pair, score_pair, 0)
    pl.when(odd)(lambda: score_tiles((n_kt - 1,)))

    n_acc = 4

    def count(pred, thr):
        def body(c, accs):
            ones = jnp.where(pred(keys_ref[c], thr), 1.0, 0.0)
            rows = [ones[8 * j:8 * j + 8] for j in range(sc_n // 8)]
            per = len(rows) // n_acc
            return tuple(a + _tree_sum(rows[k * per:(k + 1) * per]) for k, a in enumerate(accs))
        accs = lax.fori_loop(0, n_c, body, tuple(jnp.zeros((8, qb), F32) for _ in range(n_acc)))
        return jnp.sum(_tree_sum(list(accs)), axis=0, keepdims=True)

    def bit_step(it, carry):
        thr, best = carry
        cand = thr + jnp.left_shift(jnp.int32(1), 31 - it)
        cnt = count(lambda kk, t_: kk >= t_, cand)
        ok = cnt >= topk
        return jnp.where(ok, cand, thr), jnp.where(ok, cnt, best)

    thr, n_ge = lax.fori_loop(0, 32, bit_step,
                              (jnp.full((1, qb), -2 ** 31, jnp.int32), jnp.zeros((1, qb), F32)))

    overflow = jnp.max(n_ge) > topk

    @pl.when(jnp.logical_not(overflow))
    def _():
        def body(c, carry):
            sel = (keys_ref[c] >= thr) & (c * sc_n + krow < limit)
            bias_ref[c] = jnp.where(sel, 0.0, NEG_INF)
            return carry
        lax.fori_loop(0, n_kt * cpt, body, 0)

    @pl.when(overflow)
    def _():
        need = topk - count(lambda kk, t_: kk > t_, thr)
        r_i = lax.broadcasted_iota(jnp.int32, (sc_n, sc_n), 0)
        c_i = lax.broadcasted_iota(jnp.int32, (sc_n, sc_n), 1)
        lower = jnp.where(c_i <= r_i, 1.0, 0.0).astype(BF16)

        def tie_chunk(c, off):
            kk = keys_ref[c]
            eq = kk == thr
            local = jnp.dot(lower, jnp.where(eq, 1.0, 0.0).astype(BF16), preferred_element_type=F32)
            take = eq & (local + off <= need)
            sel = ((kk > thr) | take) & (c * sc_n + krow < limit)
            bias_ref[c] = jnp.where(sel, 0.0, NEG_INF)
            return off + local[sc_n - 1:sc_n, :]

        lax.fori_loop(0, n_kt * cpt, tie_chunk, jnp.zeros((1, qb), F32))

    r_e = lax.broadcasted_iota(jnp.int32, (qb, qb), 0)
    c_e = lax.broadcasted_iota(jnp.int32, (qb, qb), 1)
    eye = jnp.where(r_e == c_e, 1.0, 0.0).astype(BF16)
    kv_heads = range(N_KV_HEADS)
    qns = [jnp.concatenate(
        [qp_ref[:, (grp * n + g) * LANES:(grp * n + g + 1) * LANES] for g in range(grp)], axis=0)
        for n in kv_heads]

    def pass_a(tiles, ms):
        ms = list(ms)
        for t in tiles:
            kt = kb_ref[pl.ds(pl.multiple_of(t * KEY_TILE, KEY_TILE), KEY_TILE), :]
            bias = jnp.concatenate([bias_ref[t * cpt + j] for j in range(cpt)], axis=0)
            bias = jnp.concatenate([bias] * grp, axis=1)
            for n in kv_heads:
                s = _nt_dot(kt, qns[n]) + bias
                s_ref[n, t] = s
                ms[n] = jnp.maximum(ms[n], jnp.max(s, axis=0, keepdims=True))
        return tuple(ms)

    def pass_b(tiles, accs, ms):
        accs = list(accs)
        for t in tiles:
            for n in kv_heads:
                p = jnp.exp(s_ref[n, t] - ms[n]).astype(BF16)
                vt = vt_ref[t, n * VT_ROWS:(n + 1) * VT_ROWS, :]
                accs[n] = accs[n] + jnp.dot(vt, p, preferred_element_type=F32)
        return tuple(accs)

    ms = tuple(jnp.full((1, grp * qb), M_INIT, F32) for _ in kv_heads)
    ms = lax.fori_loop(0, n_pair, lambda pr, c: pass_a((2 * pr, 2 * pr + 1), c), ms)
    ms = lax.cond(odd, lambda c: pass_a((n_kt - 1,), c), lambda c: c, ms)
    accs = tuple(jnp.zeros((VT_ROWS, grp * qb), F32) for _ in kv_heads)
    accs = lax.fori_loop(0, n_pair, lambda pr, c: pass_b((2 * pr, 2 * pr + 1), c, ms), accs)
    accs = lax.cond(odd, lambda c: pass_b((n_kt - 1,), c, ms), lambda c: c, accs)

    for n in kv_heads:
        acc = accs[n]
        ot = (acc[:HEAD_DIM] / acc[HEAD_DIM:HEAD_DIM + 1]).astype(BF16)
        for g in range(grp):
            h = grp * n + g
            o_ref[:, h * HEAD_DIM:(h + 1) * HEAD_DIM] = _nt_dot(
                eye, ot[:, g * qb:(g + 1) * qb]).astype(o_ref.dtype)


def _attention(wit, qp, qia, kihl, kb, vt, *, qb, pos0, s_true, topk):
    b, t, _ = qp.shape
    s_pad = kihl.shape[1]
    assert t % qb == 0 and s_pad % KEY_TILE == 0
    grp = N_HEADS // N_KV_HEADS
    qblk = lambda w: pl.BlockSpec((None, qb, w), lambda bi, i: (bi, i, 0))
    kblk = lambda w: pl.BlockSpec((None, s_pad, w), lambda bi, i: (bi, 0, 0))
    return pl.pallas_call(
        functools.partial(_attn_body, qb=qb, pos0=pos0, s_true=s_true, topk=topk),
        grid=(b, t // qb),
        in_specs=[pl.BlockSpec((None, None, 8, qb), lambda bi, i: (bi, i, 0, 0)),
                  qblk(N_HEADS * LANES), qblk(IDX_HEADS * 4 * IDX_DIM),
                  kblk(4 * IDX_DIM), kblk(KV_DIM),
                  pl.BlockSpec((None, s_pad // KEY_TILE, N_KV_HEADS * VT_ROWS, KEY_TILE),
                               lambda bi, i: (bi, 0, 0, 0))],
        out_specs=qblk(ATTN_DIM),
        out_shape=jax.ShapeDtypeStruct((b, t, ATTN_DIM), BF16),
        scratch_shapes=[
            pltpu.VMEM((s_pad // SEARCH_CHUNK, SEARCH_CHUNK, qb), jnp.int32),
            pltpu.VMEM((s_pad // SEARCH_CHUNK, SEARCH_CHUNK, qb), F32),
            pltpu.VMEM((N_KV_HEADS, s_pad // KEY_TILE, KEY_TILE, grp * qb), F32),
        ],
        compiler_params=_cparams(("parallel", "arbitrary")),
        name="sparse_attn",
    )(wit, qp, qia, kihl, kb, vt)


def _merge_body(x_ref, oa_ref, ob_ref, wga_ref, wgb_ref, wpa_ref, wpb_ref, wout_ref, g_ref, b_ref, o_ref):
    x = x_ref[...]
    xb = x.astype(BF16)
    ga = jnp.dot(xb, wga_ref[...], preferred_element_type=F32)
    gb = jnp.dot(xb, wgb_ref[...], preferred_element_type=F32)
    pa = jnp.dot(oa_ref[...], wpa_ref[...], preferred_element_type=F32)
    pb = jnp.dot(ob_ref[...], wpb_ref[...], preferred_element_type=F32)
    merged = jax.nn.sigmoid(ga) * pa + jax.nn.sigmoid(gb) * pb
    h = ALPHA * x + jnp.dot(merged.astype(BF16), wout_ref[...], preferred_element_type=F32)
    o_ref[...] = _layer_norm(h, g_ref[...], b_ref[...])


def _merge(x2d, oa, ob, wga, wgb, wpa, wpb, wout, g, b, *, tm):
    m = x2d.shape[0]
    row = lambda w: pl.BlockSpec((tm, w), lambda i: (i, 0))
    full = lambda a: pl.BlockSpec(a.shape, lambda i: (0,) * a.ndim)
    return pl.pallas_call(
        _merge_body,
        grid=(m // tm,),
        in_specs=[row(D_MODEL), row(ATTN_DIM), row(GMLP_DIM), full(wga), full(wgb), full(wpa), full(wpb),
                  full(wout), full(g), full(b)],
        out_specs=row(D_MODEL),
        out_shape=jax.ShapeDtypeStruct((m, D_MODEL), F32),
        compiler_params=_cparams(("parallel",)),
        name="merge",
    )(x2d, oa, ob, wga, wgb, wpa, wpb, wout, g, b)


FF_SPLIT = 4


def _ffn_body(x_ref, w1_ref, w2_ref, g_ref, b_ref, o_ref):
    x = x_ref[...]
    xb = x.astype(BF16)
    cw = D_FF // FF_SPLIT
    ff = jnp.zeros(x.shape, F32)
    for c in range(FF_SPLIT):
        hcol = jnp.maximum(jnp.dot(xb, w1_ref[:, c * cw:(c + 1) * cw], preferred_element_type=F32), 0.0)
        ff = ff + jnp.dot((hcol * hcol).astype(BF16), w2_ref[c * cw:(c + 1) * cw, :],
                          preferred_element_type=F32)
    o_ref[...] = _layer_norm(ALPHA * x + ff, g_ref[...], b_ref[...])


def _ffn(x2d, w1, w2, g, b, *, tm):
    m = x2d.shape[0]
    row = lambda w: pl.BlockSpec((tm, w), lambda i: (i, 0))
    full = lambda a: pl.BlockSpec(a.shape, lambda i: (0,) * a.ndim)
    return pl.pallas_call(
        _ffn_body,
        grid=(m // tm,),
        in_specs=[row(D_MODEL), full(w1), full(w2), full(g), full(b)],
        out_specs=row(D_MODEL),
        out_shape=jax.ShapeDtypeStruct((m, D_MODEL), F32),
        compiler_params=_cparams(("parallel",)),
        name="ffn",
    )(x2d, w1, w2, g, b)


def _rope_tables(pos, rows):
    half = ROPE_DIM // 2
    freqs = ROPE_THETA ** (-jnp.arange(half, dtype=F32) * 2.0 / ROPE_DIM)
    ang = pos.astype(F32)[:, None] * freqs[None, :]
    cos, sin = jnp.cos(ang), jnp.sin(ang)
    t = pos.shape[0]
    rest = HEAD_DIM - ROPE_DIM
    cos_h = jnp.concatenate([cos, cos, jnp.ones((t, rest), F32)], axis=1)
    sn_h = jnp.concatenate([-sin, jnp.zeros((t, half + rest), F32)], axis=1)
    sp_h = jnp.concatenate([jnp.zeros((t, half), F32), sin, jnp.zeros((t, rest), F32)], axis=1)
    reps = (max(rows // t, 1), LANES // HEAD_DIM)
    return tuple(jnp.tile(a, reps) for a in (cos_h, sn_h, sp_h))


def _split_w_in(w):
    wb = w.astype(BF16)
    n_head = OFF_KIW + IDX_DIM + IDX_HEADS
    col = lax.broadcasted_iota(jnp.int32, (1, OFF_U), 1)
    head = jnp.where(col < n_head, wb[:, :OFF_U], jnp.zeros((), BF16))
    tail = wb[:, n_head:]
    w_a = jnp.concatenate([head, tail[:, :2 * GMLP_DIM]], axis=1)
    return w_a, tail[:, 2 * GMLP_DIM:2 * GMLP_DIM + D_MODEL], tail[:, 2 * GMLP_DIM + D_MODEL:]


def _split_hi_lo(ki):
    hi = ki.astype(BF16)
    lo = (ki - hi.astype(F32)).astype(BF16)
    return jnp.concatenate([hi, hi, lo, lo], axis=-1)


def _head_weights_t(kiw):
    b, t, _ = kiw.shape
    wi = kiw[:, :, IDX_DIM:IDX_DIM + IDX_HEADS].reshape(b, t // Q_BLOCK, Q_BLOCK, IDX_HEADS)
    return jnp.pad(wi.transpose(0, 1, 3, 2), ((0, 0), (0, 0), (0, 8 - IDX_HEADS), (0, 0)))


def _augment_vt(vt):
    b, nt, _, kt = vt.shape
    v4 = vt.reshape(b, nt, N_KV_HEADS, HEAD_DIM, kt)
    ones = jnp.ones((b, nt, N_KV_HEADS, 1, kt), vt.dtype)
    zeros = jnp.zeros((b, nt, N_KV_HEADS, VT_ROWS - HEAD_DIM - 1, kt), vt.dtype)
    return jnp.concatenate([v4, ones, zeros], axis=3).reshape(b, nt, N_KV_HEADS * VT_ROWS, kt)


def _pad_keys(a, s_pad):
    return jnp.pad(a, ((0, 0), (0, s_pad - a.shape[1]), (0, 0)))


def kernel(x_prompt, x_sample, cache_k, cache_v, cache_idx_k, w_in, idx_k_g, idx_k_b, sgu_ln_g, sgu_ln_b,
           w_s, b_s, w_pa, w_pb, w_out, ln1_g, ln1_b, w_ff1, w_ff2, ln2_g, ln2_b):
    bp, tp, _ = x_prompt.shape
    bs, ts, _ = x_sample.shape
    depth = w_in.shape[0]
    past = cache_k.shape[2]
    topk_p = min(TOPK_MAX, tp // 4)
    topk_s = min(TOPK_MAX, (past + ts) // 4)
    tm_p = min(512, bp * tp)
    tm_s = bs * ts
    s_all = past + ts
    s_pad = -(-s_all // KEY_TILE) * KEY_TILE

    tabs_p = _rope_tables(jnp.arange(tp), tm_p)
    tabs_s = _rope_tables(past + jnp.arange(ts), tm_s)

    xp = x_prompt.reshape(bp * tp, D_MODEL)
    xs = x_sample.reshape(bs * ts, D_MODEL)
    pk, pv, pik, sk, sv, sik, ssv = [], [], [], [], [], [], []
    row2 = lambda a: a.reshape(1, -1)
    for l in range(depth):
        w_a, w_ga, w_gb = _split_w_in(w_in[l])
        kig = row2(jnp.pad(idx_k_g[l], (0, LANES - IDX_DIM)))
        kib = row2(jnp.pad(idx_k_b[l], (0, LANES - IDX_DIM)))
        sg, sb = row2(sgu_ln_g[l]), row2(sgu_ln_b[l])
        wpa, wpb, wout = w_pa[l].astype(BF16), w_pb[l].astype(BF16), w_out[l].astype(BF16)
        w1, w2 = w_ff1[l].astype(BF16), w_ff2[l].astype(BF16)
        g1, b1, g2, b2 = row2(ln1_g[l]), row2(ln1_b[l]), row2(ln2_g[l]), row2(ln2_b[l])

        qp, kf, vf, kb, vt, qia, kiw, kihl, ob = _inproj(
            xp, w_a, tabs_p, kig, kib, sg, sb, w_s[l], b_s[l].T, tm=tm_p, cl=GMLP_CHUNK, emit_vn=False,
            v_transposed=True)
        r3 = lambda a: a.reshape(bp, tp, a.shape[-1])
        oa = _attention(_head_weights_t(r3(kiw)), r3(qp), r3(qia), r3(kihl), r3(kb),
                        vt.reshape(bp, tp // KEY_TILE, N_KV_HEADS * VT_ROWS, KEY_TILE),
                        qb=Q_BLOCK, pos0=0, s_true=tp, topk=topk_p)
        x1 = _merge(xp, oa.reshape(bp * tp, ATTN_DIM), ob, w_ga, w_gb, wpa, wpb, wout, g1, b1, tm=tm_p)
        xp = _ffn(x1, w1, w2, g2, b2, tm=tm_p)
        pk.append(kf.reshape(bp, tp, N_KV_HEADS, HEAD_DIM))
        pv.append(vf.reshape(bp, tp, N_KV_HEADS, HEAD_DIM))
        pik.append(kiw[:, :IDX_DIM].reshape(bp, tp, IDX_DIM))

        qp, kf, vf, kb, vb, qia, kiw, kihl, ob, vn = _inproj(
            xs, w_a, tabs_s, kig, kib, sg, sb, w_s[l][:, :ts, :ts], b_s[l][:, :ts].T,
            tm=tm_s, cl=ts, emit_vn=True, v_transposed=False)
        r3 = lambda a: a.reshape(bs, ts, a.shape[-1])
        k_all = _pad_keys(jnp.concatenate([cache_k[l].reshape(bs, past, KV_DIM).astype(BF16), r3(kb)], 1), s_pad)
        v_all = _pad_keys(jnp.concatenate([cache_v[l].reshape(bs, past, KV_DIM).astype(BF16), r3(vb)], 1), s_pad)
        vt_all = _augment_vt(v_all.reshape(bs, s_pad // KEY_TILE, KEY_TILE, KV_DIM).transpose(0, 1, 3, 2))
        ki_all = _pad_keys(jnp.concatenate([_split_hi_lo(cache_idx_k[l]), r3(kihl)], 1), s_pad)
        padq = lambda a: jnp.pad(r3(a), ((0, 0), (0, Q_BLOCK - ts), (0, 0)))
        oa = _attention(_head_weights_t(padq(kiw)), padq(qp), padq(qia), ki_all, k_all, vt_all,
                        qb=Q_BLOCK, pos0=past, s_true=s_all, topk=topk_s)[:, :ts]
        x1 = _merge(xs, oa.reshape(bs * ts, ATTN_DIM), ob, w_ga, w_gb, wpa, wpb, wout, g1, b1, tm=tm_s)
        xs = _ffn(x1, w1, w2, g2, b2, tm=tm_s)
        sk.append(kf.reshape(bs, ts, N_KV_HEADS, HEAD_DIM))
        sv.append(vf.reshape(bs, ts, N_KV_HEADS, HEAD_DIM))
        sik.append(kiw[:, :IDX_DIM].reshape(bs, ts, IDX_DIM))
        ssv.append(vn.reshape(bs, ts, GMLP_DIM))

    return (xp.reshape(bp, tp, D_MODEL), xs.reshape(bs, ts, D_MODEL),
            jnp.stack(pk), jnp.stack(pv), jnp.stack(pik),
            jnp.stack(sk), jnp.stack(sv), jnp.stack(sik), jnp.stack(ssv))
```

```python
import functools

import jax
import jax.numpy as jnp
from jax import lax
from jax.experimental import pallas as pl
from jax.experimental.pallas import tpu as pltpu

F32 = jnp.float32
BF16 = jnp.bfloat16

D_MODEL = 1024
N_HEADS = 8
N_KV_HEADS = 2
HEAD_DIM = 64
ATTN_DIM = N_HEADS * HEAD_DIM
KV_DIM = N_KV_HEADS * HEAD_DIM
IDX_HEADS = 4
IDX_DIM = 64
TOPK_MAX = 256
CHUNK = 64
GMLP_CHUNK = 128
GMLP_GROUPS = 4
GMLP_DIM = 512
GMLP_GROUP_DIM = GMLP_DIM // GMLP_GROUPS
D_FF = 4 * D_MODEL
ROPE_THETA = 500000.0
ROPE_DIM = HEAD_DIM // 4
DEPTH = 2
ALPHA = (2 * DEPTH) ** 0.25
LN_EPS = 1e-5

LANES = 128
Q_SCALE = HEAD_DIM ** -0.5
IDX_SCALE = IDX_DIM ** -0.5
WI_SCALE = IDX_HEADS ** -0.5
NEG_INF = float("-inf")
M_INIT = -1e30

OFF_Q, OFF_K, OFF_V, OFF_QI, OFF_KIW, OFF_U, OFF_VG = 0, 512, 640, 768, 1024, 1152, 1664
IN_COLS = 2176
KEY_TILE = 512
Q_BLOCK = LANES
SEARCH_CHUNK = 512
VT_ROWS = 80
VMEM_LIMIT = 56 * 1024 * 1024


def _cparams(sem):
    return pltpu.CompilerParams(dimension_semantics=sem, vmem_limit_bytes=VMEM_LIMIT)


def _layer_norm(x, g, b):
    mu = jnp.mean(x, axis=-1, keepdims=True)
    d = x - mu
    var = jnp.mean(d * d, axis=-1, keepdims=True)
    return d * lax.rsqrt(var + LN_EPS) * g + b


def _rope(y, cos, sn, sp):
    w = y.shape[-1]
    half = ROPE_DIM // 2
    return y * cos + pltpu.roll(y, w - half, 1) * sn + pltpu.roll(y, half, 1) * sp


def _tile_lanes(a, n):
    return a if n == 1 else jnp.concatenate([a] * n, axis=1)


def _inproj_body(x_ref, w_ref, cos_ref, sn_ref, sp_ref, kig_ref, kib_ref, sg_ref, sb_ref,
                 ws_ref, bst_ref,
                 qp_ref, kf_ref, vf_ref, kb_ref, vb_ref, qia_ref, kiw_ref, kihl_ref, ob_ref,
                 *vn_refs, cl, v_transposed):
    tm = x_ref.shape[0]
    xb = x_ref[...].astype(BF16)

    def proj(lo, n):
        return jnp.dot(xb, w_ref[:, lo:lo + n], preferred_element_type=F32)

    cos, sn, sp = cos_ref[...], sn_ref[...], sp_ref[...]
    lane = lax.broadcasted_iota(jnp.int32, (tm, LANES), 1)
    left = lane < IDX_DIM

    q = _rope(proj(OFF_Q, ATTN_DIM), _tile_lanes(cos, 4), _tile_lanes(sn, 4), _tile_lanes(sp, 4)) * Q_SCALE
    heads_per_kv = N_HEADS // N_KV_HEADS
    for j in range(ATTN_DIM // LANES):
        tile = q[:, j * LANES:(j + 1) * LANES]
        rolled = pltpu.roll(tile, HEAD_DIM, 1)
        on_left = (2 * j) // heads_per_kv == 0
        keep = left if on_left else jnp.logical_not(left)
        even = jnp.where(keep, tile if on_left else rolled, 0.0)
        odd = jnp.where(keep, rolled if on_left else tile, 0.0)
        qp_ref[:, (2 * j) * LANES:(2 * j + 1) * LANES] = even.astype(BF16)
        qp_ref[:, (2 * j + 1) * LANES:(2 * j + 2) * LANES] = odd.astype(BF16)

    k = _rope(proj(OFF_K, KV_DIM), cos, sn, sp)
    kf_ref[...] = k
    kb_ref[...] = k.astype(BF16)
    v = proj(OFF_V, KV_DIM)
    vf_ref[...] = v
    vb = v.astype(BF16)
    if v_transposed:
        r_e = lax.broadcasted_iota(jnp.int32, (N_KV_HEADS * VT_ROWS, LANES), 0)
        c_e = lax.broadcasted_iota(jnp.int32, (N_KV_HEADS * VT_ROWS, LANES), 1)
        head, d = r_e // VT_ROWS, r_e % VT_ROWS
        sel = jnp.where((d < HEAD_DIM) & (c_e == head * HEAD_DIM + d), 1.0, 0.0).astype(BF16)
        vt = lax.dot_general(sel, vb, (((1,), (1,)), ((), ())), preferred_element_type=F32)
        r_o = lax.broadcasted_iota(jnp.int32, (N_KV_HEADS * VT_ROWS, tm), 0)
        vb_ref[...] = jnp.where(r_o % VT_ROWS == HEAD_DIM, 1.0, vt).astype(BF16)
    else:
        vb_ref[...] = vb

    qi = _rope(proj(OFF_QI, IDX_HEADS * IDX_DIM), _tile_lanes(cos, 2), _tile_lanes(sn, 2),
               _tile_lanes(sp, 2)) * IDX_SCALE
    qi_hi = qi.astype(BF16).astype(F32)
    qi_lo = qi - qi_hi
    for t in range(IDX_HEADS // 2):
        h_t = qi_hi[:, t * LANES:(t + 1) * LANES]
        l_t = qi_lo[:, t * LANES:(t + 1) * LANES]
        a0 = jnp.where(left, h_t, pltpu.roll(l_t, IDX_DIM, 1))
        a1 = jnp.where(left, pltpu.roll(h_t, IDX_DIM, 1), l_t)
        for j, a in enumerate((a0, a1)):
            base = (2 * t + j) * 2 * LANES
            qia_ref[:, base:base + 2 * LANES] = jnp.concatenate([a, a], axis=1).astype(BF16)

    slab = proj(OFF_KIW, LANES)
    mu = jnp.sum(jnp.where(left, slab, 0.0), axis=1, keepdims=True) * (1.0 / IDX_DIM)
    d = jnp.where(left, slab - mu, 0.0)
    var = jnp.sum(d * d, axis=1, keepdims=True) * (1.0 / IDX_DIM)
    kin = d * lax.rsqrt(var + LN_EPS) * kig_ref[...] + kib_ref[...]
    slab = jnp.where(left, kin, slab)
    slab = _rope(slab, jnp.where(left, cos, 1.0), jnp.where(left, sn, 0.0), jnp.where(left, sp, 0.0))
    kiw_ref[...] = slab
    kz = jnp.where(left, slab, 0.0)
    k_hi = kz.astype(BF16).astype(F32)
    k_lo = kz - k_hi
    hh = k_hi + pltpu.roll(k_hi, IDX_DIM, 1)
    ll = k_lo + pltpu.roll(k_lo, IDX_DIM, 1)
    kihl_ref[...] = jnp.concatenate([hh, ll], axis=1).astype(BF16)

    u = proj(OFF_U, GMLP_DIM)
    vn = _layer_norm(proj(OFF_VG, GMLP_DIM), sg_ref[...], sb_ref[...])
    if vn_refs:
        vn_refs[0][...] = vn
    vnb = vn.astype(BF16)
    r_i = lax.broadcasted_iota(jnp.int32, (cl, cl), 0)
    c_i = lax.broadcasted_iota(jnp.int32, (cl, cl), 1)
    for g in range(GMLP_GROUPS):
        wg = jnp.where(r_i >= c_i, ws_ref[g], 0.0).astype(BF16)
        bg = bst_ref[:, g:g + 1]
        gs = slice(g * GMLP_GROUP_DIM, (g + 1) * GMLP_GROUP_DIM)
        for c in range(tm // cl):
            rs = slice(c * cl, (c + 1) * cl)
            mix = jnp.dot(wg, vnb[rs, gs], preferred_element_type=F32) + bg
            ob_ref[rs, gs] = (u[rs, gs] * mix).astype(BF16)


def _inproj(x2d, w_a, tabs, kig, kib, sg, sb, ws, bst, *, tm, cl, emit_vn, v_transposed):
    m = x2d.shape[0]
    cos, sn, sp = tabs
    nt = cos.shape[0] // tm
    row = lambda w: pl.BlockSpec((tm, w), lambda i: (i, 0))
    tab = pl.BlockSpec((tm, LANES), lambda i: (i % nt, 0))
    full = lambda a: pl.BlockSpec(a.shape, lambda i: (0,) * a.ndim)
    if v_transposed:
        vb_shape = jax.ShapeDtypeStruct((m // tm, N_KV_HEADS * VT_ROWS, tm), BF16)
        vb_spec = pl.BlockSpec((None, N_KV_HEADS * VT_ROWS, tm), lambda i: (i, 0, 0))
    else:
        vb_shape, vb_spec = jax.ShapeDtypeStruct((m, KV_DIM), BF16), row(KV_DIM)
    out_shapes = [
        jax.ShapeDtypeStruct((m, N_HEADS * LANES), BF16),
        jax.ShapeDtypeStruct((m, KV_DIM), F32),
        jax.ShapeDtypeStruct((m, KV_DIM), F32),
        jax.ShapeDtypeStruct((m, KV_DIM), BF16),
        vb_shape,
        jax.ShapeDtypeStruct((m, IDX_HEADS * 4 * IDX_DIM), BF16),
        jax.ShapeDtypeStruct((m, LANES), F32),
        jax.ShapeDtypeStruct((m, 4 * IDX_DIM), BF16),
        jax.ShapeDtypeStruct((m, GMLP_DIM), BF16),
    ]
    out_specs = [row(N_HEADS * LANES), row(KV_DIM), row(KV_DIM), row(KV_DIM), vb_spec,
                 row(IDX_HEADS * 4 * IDX_DIM), row(LANES), row(4 * IDX_DIM), row(GMLP_DIM)]
    if emit_vn:
        out_shapes.append(jax.ShapeDtypeStruct((m, GMLP_DIM), F32))
        out_specs.append(row(GMLP_DIM))
    return pl.pallas_call(
        functools.partial(_inproj_body, cl=cl, v_transposed=v_transposed),
        grid=(m // tm,),
        in_specs=[row(D_MODEL), full(w_a), tab, tab, tab, full(kig), full(kib), full(sg), full(sb),
                  full(ws), full(bst)],
        out_specs=out_specs,
        out_shape=out_shapes,
        compiler_params=_cparams(("parallel",)),
        name="inproj",
    )(x2d, w_a, cos, sn, sp, kig, kib, sg, sb, ws, bst)


def _nt_dot(a, b):
    return lax.dot_general(a, b, (((1,), (1,)), ((), ())), preferred_element_type=F32)


def _tree_sum(parts):
    while len(parts) > 1:
        parts = [parts[j] + parts[j + 1] for j in range(0, len(parts) - 1, 2)] + (
            [parts[-1]] if len(parts) % 2 else [])
    return parts[0]


def _attn_body(wit_ref, qp_ref, qia_ref, kihl_ref, kb_ref, vt_ref, o_ref,
               keys_ref, bias_ref, s_ref, *, qb, n_valid, pos0, s_true, topk):
    i = pl.program_id(1)
    sc_n = SEARCH_CHUNK
    cpt = KEY_TILE // sc_n
    grp = N_HEADS // N_KV_HEADS
    qlane = lax.broadcasted_iota(jnp.int32, (1, qb), 1)
    qpos = pos0 + i * qb + qlane
    limit = jnp.minimum((qpos // CHUNK + 1) * CHUNK, s_true)
    kv_len = jnp.minimum(((pos0 + i * qb + qb - 1) // CHUNK + 1) * CHUNK, s_true)
    n_kt = (kv_len + KEY_TILE - 1) // KEY_TILE
    n_c = (kv_len + sc_n - 1) // sc_n
    krow = lax.broadcasted_iota(jnp.int32, (sc_n, 1), 0)

    wit = wit_ref[...] * WI_SCALE
    hw = 4 * IDX_DIM
    pairs = [jnp.concatenate([qia_ref[:, (2 * hp) * hw:(2 * hp + 1) * hw],
                              qia_ref[:, (2 * hp + 1) * hw:(2 * hp + 2) * hw]], axis=0)
             for hp in range(IDX_HEADS // 2)]

    def score_tiles(tiles):
        for t in tiles:
            kt = kihl_ref[pl.ds(pl.multiple_of(t * KEY_TILE, KEY_TILE), KEY_TILE), :]
            sc = jnp.zeros((KEY_TILE, qb), F32)
            for hp, a2 in enumerate(pairs):
                dots = jnp.maximum(_nt_dot(kt, a2), 0.0)
                sc = (sc + wit[2 * hp:2 * hp + 1, :] * dots[:, :qb]
                      + wit[2 * hp + 1:2 * hp + 2, :] * dots[:, qb:])
            for j in range(cpt):
                kidx = t * KEY_TILE + j * sc_n + krow
                s_j = jnp.where(kidx < limit, sc[j * sc_n:(j + 1) * sc_n], NEG_INF)
                s_j = jnp.where(s_j == 0.0, 0.0, s_j)
                bits = lax.bitcast_convert_type(s_j, jnp.int32)
                keys_ref[t * cpt + j] = bits ^ ((bits >> 31) & 0x7FFFFFFF)

    n_pair = n_kt // 2
    odd = n_kt % 2 == 1

    def score_pair(pr, carry):
        score_tiles((2 * pr, 2 * pr + 1))
        return carry

    lax.fori_loop(0, n_pair, score_pair, 0)
    pl.when(odd)(lambda: score_tiles((n_kt - 1,)))

    n_acc = 4

    def count(pred, thr):
        def body(c, accs):
            ones = jnp.where(pred(keys_ref[c], thr), 1.0, 0.0)
            rows = [ones[8 * j:8 * j + 8] for j in range(sc_n // 8)]
            per = len(rows) // n_acc
            return tuple(a + _tree_sum(rows[k * per:(k + 1) * per]) for k, a in enumerate(accs))
        accs = lax.fori_loop(0, n_c, body, tuple(jnp.zeros((8, qb), F32) for _ in range(n_acc)))
        return jnp.sum(_tree_sum(list(accs)), axis=0, keepdims=True)

    def bit_step(it, carry):
        thr, best = carry
        cand = thr + jnp.left_shift(jnp.int32(1), 31 - it)
        cnt = count(lambda kk, t_: kk >= t_, cand)
        ok = cnt >= topk
        return jnp.where(ok, cand, thr), jnp.where(ok, cnt, best)

    thr, n_ge = lax.fori_loop(0, 32, bit_step,
                              (jnp.full((1, qb), -2 ** 31, jnp.int32), jnp.zeros((1, qb), F32)))

    real = qlane < n_valid
    overflow = jnp.max(jnp.where(real, n_ge, 0.0)) > topk

    @pl.when(jnp.logical_not(overflow))
    def _():
        def body(c, carry):
            sel = (keys_ref[c] >= thr) & (c * sc_n + krow < limit)
            bias_ref[c] = jnp.where(sel, 0.0, NEG_INF)
            return carry
        lax.fori_loop(0, n_kt * cpt, body, 0)

    @pl.when(overflow)
    def _():
        need = topk - count(lambda kk, t_: kk > t_, thr)
        r_i = lax.broadcasted_iota(jnp.int32, (sc_n, sc_n), 0)
        c_i = lax.broadcasted_iota(jnp.int32, (sc_n, sc_n), 1)
        lower = jnp.where(c_i <= r_i, 1.0, 0.0).astype(BF16)

        def tie_chunk(c, off):
            kk = keys_ref[c]
            eq = kk == thr
            local = jnp.dot(lower, jnp.where(eq, 1.0, 0.0).astype(BF16), preferred_element_type=F32)
            take = eq & (local + off <= need)
            sel = ((kk > thr) | take) & (c * sc_n + krow < limit)
            bias_ref[c] = jnp.where(sel, 0.0, NEG_INF)
            return off + local[sc_n - 1:sc_n, :]

        lax.fori_loop(0, n_kt * cpt, tie_chunk, jnp.zeros((1, qb), F32))

    r_e = lax.broadcasted_iota(jnp.int32, (qb, qb), 0)
    c_e = lax.broadcasted_iota(jnp.int32, (qb, qb), 1)
    eye = jnp.where(r_e == c_e, 1.0, 0.0).astype(BF16)
    kv_heads = range(N_KV_HEADS)
    qns = [jnp.concatenate(
        [qp_ref[:, (grp * n + g) * LANES:(grp * n + g + 1) * LANES] for g in range(grp)], axis=0)
        for n in kv_heads]

    def pass_a(tiles, ms):
        ms = list(ms)
        for t in tiles:
            kt = kb_ref[pl.ds(pl.multiple_of(t * KEY_TILE, KEY_TILE), KEY_TILE), :]
            bias = jnp.concatenate([bias_ref[t * cpt + j] for j in range(cpt)], axis=0)
            bias = jnp.concatenate([bias] * grp, axis=1)
            for n in kv_heads:
                s = _nt_dot(kt, qns[n]) + bias
                s_ref[n, t] = s
                ms[n] = jnp.maximum(ms[n], jnp.max(s, axis=0, keepdims=True))
        return tuple(ms)

    def pass_b(tiles, accs, ms):
        accs = list(accs)
        for t in tiles:
            for n in kv_heads:
                p = jnp.exp(s_ref[n, t] - ms[n]).astype(BF16)
                vt = vt_ref[t, n * VT_ROWS:(n + 1) * VT_ROWS, :]
                accs[n] = accs[n] + jnp.dot(vt, p, preferred_element_type=F32)
        return tuple(accs)

    ms = tuple(jnp.full((1, grp * qb), M_INIT, F32) for _ in kv_heads)
    ms = lax.fori_loop(0, n_pair, lambda pr, c: pass_a((2 * pr, 2 * pr + 1), c), ms)
    ms = lax.cond(odd, lambda c: pass_a((n_kt - 1,), c), lambda c: c, ms)
    accs = tuple(jnp.zeros((VT_ROWS, grp * qb), F32) for _ in kv_heads)
    accs = lax.fori_loop(0, n_pair, lambda pr, c: pass_b((2 * pr, 2 * pr + 1), c, ms), accs)
    accs = lax.cond(odd, lambda c: pass_b((n_kt - 1,), c, ms), lambda c: c, accs)

    for n in kv_heads:
        acc = accs[n]
        ot = (acc[:HEAD_DIM] / acc[HEAD_DIM:HEAD_DIM + 1]).astype(BF16)
        for g in range(grp):
            h = grp * n + g
            o_ref[:, h * HEAD_DIM:(h + 1) * HEAD_DIM] = _nt_dot(
                eye, ot[:, g * qb:(g + 1) * qb]).astype(o_ref.dtype)


def _attention(wit, qp, qia, kihl, kb, vt, *, qb, n_valid, pos0, s_true, topk):
    b, t, _ = qp.shape
    s_pad = kihl.shape[1]
    assert t % qb == 0 and s_pad % KEY_TILE == 0
    grp = N_HEADS // N_KV_HEADS
    qblk = lambda w: pl.BlockSpec((None, qb, w), lambda bi, i: (bi, i, 0))
    kblk = lambda w: pl.BlockSpec((None, s_pad, w), lambda bi, i: (bi, 0, 0))
    return pl.pallas_call(
        functools.partial(_attn_body, qb=qb, n_valid=n_valid, pos0=pos0, s_true=s_true, topk=topk),
        grid=(b, t // qb),
        in_specs=[pl.BlockSpec((None, None, 8, qb), lambda bi, i: (bi, i, 0, 0)),
                  qblk(N_HEADS * LANES), qblk(IDX_HEADS * 4 * IDX_DIM),
                  kblk(4 * IDX_DIM), kblk(KV_DIM),
                  pl.BlockSpec((None, s_pad // KEY_TILE, N_KV_HEADS * VT_ROWS, KEY_TILE),
                               lambda bi, i: (bi, 0, 0, 0))],
        out_specs=qblk(ATTN_DIM),
        out_shape=jax.ShapeDtypeStruct((b, t, ATTN_DIM), BF16),
        scratch_shapes=[
            pltpu.VMEM((s_pad // SEARCH_CHUNK, SEARCH_CHUNK, qb), jnp.int32),
            pltpu.VMEM((s_pad // SEARCH_CHUNK, SEARCH_CHUNK, qb), F32),
            pltpu.VMEM((N_KV_HEADS, s_pad // KEY_TILE, KEY_TILE, grp * qb), F32),
        ],
        compiler_params=_cparams(("parallel", "arbitrary")),
        name="sparse_attn",
    )(wit, qp, qia, kihl, kb, vt)


FF_SPLIT = 4


def _post_body(x_ref, oa_ref, ob_ref, wga_ref, wgb_ref, wpa_ref, wpb_ref, wout_ref, g1_ref, b1_ref,
               w1_ref, w2_ref, g2_ref, b2_ref, o_ref):
    x = x_ref[...]
    xb = x.astype(BF16)
    ga = jnp.dot(xb, wga_ref[...], preferred_element_type=F32)
    gb = jnp.dot(xb, wgb_ref[...], preferred_element_type=F32)
    pa = jnp.dot(oa_ref[...], wpa_ref[...], preferred_element_type=F32)
    pb = jnp.dot(ob_ref[...], wpb_ref[...], preferred_element_type=F32)
    merged = jax.nn.sigmoid(ga) * pa + jax.nn.sigmoid(gb) * pb
    h = ALPHA * x + jnp.dot(merged.astype(BF16), wout_ref[...], preferred_element_type=F32)
    x1 = _layer_norm(h, g1_ref[...], b1_ref[...])
    x1b = x1.astype(BF16)
    cw = D_FF // FF_SPLIT
    ff = jnp.zeros(x.shape, F32)
    for c in range(FF_SPLIT):
        hcol = jnp.maximum(jnp.dot(x1b, w1_ref[:, c * cw:(c + 1) * cw], preferred_element_type=F32), 0.0)
        ff = ff + jnp.dot((hcol * hcol).astype(BF16), w2_ref[c * cw:(c + 1) * cw, :],
                          preferred_element_type=F32)
    o_ref[...] = _layer_norm(ALPHA * x1 + ff, g2_ref[...], b2_ref[...])


def _post_block(x2d, oa, ob, wga, wgb, wpa, wpb, wout, g1, b1, w1, w2, g2, b2, *, tm):
    m = x2d.shape[0]
    row = lambda w: pl.BlockSpec((tm, w), lambda i: (i, 0))
    full = lambda a: pl.BlockSpec(a.shape, lambda i: (0,) * a.ndim, pipeline_mode=pl.Buffered(1))
    consts = (wga, wgb, wpa, wpb, wout, g1, b1, w1, w2, g2, b2)
    return pl.pallas_call(
        _post_body,
        grid=(m // tm,),
        in_specs=[row(D_MODEL), row(ATTN_DIM), row(GMLP_DIM)] + [full(a) for a in consts],
        out_specs=row(D_MODEL),
        out_shape=jax.ShapeDtypeStruct((m, D_MODEL), F32),
        compiler_params=_cparams(("parallel",)),
        name="post_block",
    )(x2d, oa, ob, *consts)


def _rope_tables(pos, rows):
    half = ROPE_DIM // 2
    freqs = ROPE_THETA ** (-jnp.arange(half, dtype=F32) * 2.0 / ROPE_DIM)
    ang = pos.astype(F32)[:, None] * freqs[None, :]
    cos, sin = jnp.cos(ang), jnp.sin(ang)
    t = pos.shape[0]
    rest = HEAD_DIM - ROPE_DIM
    cos_h = jnp.concatenate([cos, cos, jnp.ones((t, rest), F32)], axis=1)
    sn_h = jnp.concatenate([-sin, jnp.zeros((t, half + rest), F32)], axis=1)
    sp_h = jnp.concatenate([jnp.zeros((t, half), F32), sin, jnp.zeros((t, rest), F32)], axis=1)
    reps = (max(rows // t, 1), LANES // HEAD_DIM)
    return tuple(jnp.tile(a, reps) for a in (cos_h, sn_h, sp_h))


def _split_w_in(w):
    wb = w.astype(BF16)
    n_head = OFF_KIW + IDX_DIM + IDX_HEADS
    col = lax.broadcasted_iota(jnp.int32, (1, OFF_U), 1)
    head = jnp.where(col < n_head, wb[:, :OFF_U], jnp.zeros((), BF16))
    tail = wb[:, n_head:]
    w_a = jnp.concatenate([head, tail[:, :2 * GMLP_DIM]], axis=1)
    return w_a, tail[:, 2 * GMLP_DIM:2 * GMLP_DIM + D_MODEL], tail[:, 2 * GMLP_DIM + D_MODEL:]


def _split_hi_lo(ki):
    hi = ki.astype(BF16)
    lo = (ki - hi.astype(F32)).astype(BF16)
    return jnp.concatenate([hi, hi, lo, lo], axis=-1)


def _head_weights_t(kiw):
    b, t, _ = kiw.shape
    wi = kiw[:, :, IDX_DIM:IDX_DIM + IDX_HEADS].reshape(b, t // Q_BLOCK, Q_BLOCK, IDX_HEADS)
    return jnp.pad(wi.transpose(0, 1, 3, 2), ((0, 0), (0, 0), (0, 8 - IDX_HEADS), (0, 0)))


def _augment_vt(vt):
    b, nt, _, kt = vt.shape
    v4 = vt.reshape(b, nt, N_KV_HEADS, HEAD_DIM, kt)
    ones = jnp.ones((b, nt, N_KV_HEADS, 1, kt), vt.dtype)
    zeros = jnp.zeros((b, nt, N_KV_HEADS, VT_ROWS - HEAD_DIM - 1, kt), vt.dtype)
    return jnp.concatenate([v4, ones, zeros], axis=3).reshape(b, nt, N_KV_HEADS * VT_ROWS, kt)


def _pad_keys(a, s_pad):
    return jnp.pad(a, ((0, 0), (0, s_pad - a.shape[1]), (0, 0)))


def kernel(x_prompt, x_sample, cache_k, cache_v, cache_idx_k, w_in, idx_k_g, idx_k_b, sgu_ln_g, sgu_ln_b,
           w_s, b_s, w_pa, w_pb, w_out, ln1_g, ln1_b, w_ff1, w_ff2, ln2_g, ln2_b):
    bp, tp, _ = x_prompt.shape
    bs, ts, _ = x_sample.shape
    depth = w_in.shape[0]
    past = cache_k.shape[2]
    topk_p = min(TOPK_MAX, tp // 4)
    topk_s = min(TOPK_MAX, (past + ts) // 4)
    tm_p = min(512, bp * tp)
    tm_s = bs * ts
    s_all = past + ts
    s_pad = -(-s_all // KEY_TILE) * KEY_TILE

    tabs_p = _rope_tables(jnp.arange(tp), tm_p)
    tabs_s = _rope_tables(past + jnp.arange(ts), tm_s)

    xp = x_prompt.reshape(bp * tp, D_MODEL)
    xs = x_sample.reshape(bs * ts, D_MODEL)
    pk, pv, pik, sk, sv, sik, ssv = [], [], [], [], [], [], []
    row2 = lambda a: a.reshape(1, -1)
    for l in range(depth):
        w_a, w_ga, w_gb = _split_w_in(w_in[l])
        kig = row2(jnp.pad(idx_k_g[l], (0, LANES - IDX_DIM)))
        kib = row2(jnp.pad(idx_k_b[l], (0, LANES - IDX_DIM)))
        sg, sb = row2(sgu_ln_g[l]), row2(sgu_ln_b[l])
        wpa, wpb, wout = w_pa[l].astype(BF16), w_pb[l].astype(BF16), w_out[l].astype(BF16)
        w1, w2 = w_ff1[l].astype(BF16), w_ff2[l].astype(BF16)
        g1, b1, g2, b2 = row2(ln1_g[l]), row2(ln1_b[l]), row2(ln2_g[l]), row2(ln2_b[l])

        qp, kf, vf, kb, vt, qia, kiw, kihl, ob = _inproj(
            xp, w_a, tabs_p, kig, kib, sg, sb, w_s[l], b_s[l].T, tm=tm_p, cl=GMLP_CHUNK, emit_vn=False,
            v_transposed=True)
        r3 = lambda a: a.reshape(bp, tp, a.shape[-1])
        oa = _attention(_head_weights_t(r3(kiw)), r3(qp), r3(qia), r3(kihl), r3(kb),
                        vt.reshape(bp, tp // KEY_TILE, N_KV_HEADS * VT_ROWS, KEY_TILE),
                        qb=Q_BLOCK, n_valid=Q_BLOCK, pos0=0, s_true=tp, topk=topk_p)
        xp = _post_block(xp, oa.reshape(bp * tp, ATTN_DIM), ob, w_ga, w_gb, wpa, wpb, wout, g1, b1,
                         w1, w2, g2, b2, tm=tm_p)
        pk.append(kf.reshape(bp, tp, N_KV_HEADS, HEAD_DIM))
        pv.append(vf.reshape(bp, tp, N_KV_HEADS, HEAD_DIM))
        pik.append(kiw[:, :IDX_DIM].reshape(bp, tp, IDX_DIM))

        qp, kf, vf, kb, vb, qia, kiw, kihl, ob, vn = _inproj(
            xs, w_a, tabs_s, kig, kib, sg, sb, w_s[l][:, :ts, :ts], b_s[l][:, :ts].T,
            tm=tm_s, cl=ts, emit_vn=True, v_transposed=False)
        r3 = lambda a: a.reshape(bs, ts, a.shape[-1])
        k_all = _pad_keys(jnp.concatenate([cache_k[l].reshape(bs, past, KV_DIM).astype(BF16), r3(kb)], 1), s_pad)
        v_all = _pad_keys(jnp.concatenate([cache_v[l].reshape(bs, past, KV_DIM).astype(BF16), r3(vb)], 1), s_pad)
        vt_all = _augment_vt(v_all.reshape(bs, s_pad // KEY_TILE, KEY_TILE, KV_DIM).transpose(0, 1, 3, 2))
        ki_all = _pad_keys(jnp.concatenate([_split_hi_lo(cache_idx_k[l]), r3(kihl)], 1), s_pad)
        padq = lambda a: jnp.pad(r3(a), ((0, 0), (0, Q_BLOCK - ts), (0, 0)))
        oa = _attention(_head_weights_t(padq(kiw)), padq(qp), padq(qia), ki_all, k_all, vt_all,
                        qb=Q_BLOCK, n_valid=ts, pos0=past, s_true=s_all, topk=topk_s)[:, :ts]
        xs = _post_block(xs, oa.reshape(bs * ts, ATTN_DIM), ob, w_ga, w_gb, wpa, wpb, wout, g1, b1,
                         w1, w2, g2, b2, tm=tm_s)
        sk.append(kf.reshape(bs, ts, N_KV_HEADS, HEAD_DIM))
        sv.append(vf.reshape(bs, ts, N_KV_HEADS, HEAD_DIM))
        sik.append(kiw[:, :IDX_DIM].reshape(bs, ts, IDX_DIM))
        ssv.append(vn.reshape(bs, ts, GMLP_DIM))

    return (xp.reshape(bp, tp, D_MODEL), xs.reshape(bs, ts, D_MODEL),
            jnp.stack(pk), jnp.stack(pv), jnp.stack(pik),
            jnp.stack(sk), jnp.stack(sv), jnp.stack(sik), jnp.stack(ssv))
```

```python
import functools

import jax
import jax.numpy as jnp
from jax import lax
from jax.experimental import pallas as pl
from jax.experimental.pallas import tpu as pltpu

F32 = jnp.float32
BF16 = jnp.bfloat16

D_MODEL = 1024
N_HEADS = 8
N_KV_HEADS = 2
HEAD_DIM = 64
ATTN_DIM = N_HEADS * HEAD_DIM
KV_DIM = N_KV_HEADS * HEAD_DIM
IDX_HEADS = 4
IDX_DIM = 64
TOPK_MAX = 256
CHUNK = 64
GMLP_CHUNK = 128
GMLP_GROUPS = 4
GMLP_DIM = 512
GMLP_GROUP_DIM = GMLP_DIM // GMLP_GROUPS
D_FF = 4 * D_MODEL
ROPE_THETA = 500000.0
ROPE_DIM = HEAD_DIM // 4
DEPTH = 2
ALPHA = (2 * DEPTH) ** 0.25
LN_EPS = 1e-5

LANES = 128
SUBLANES = 8
Q_SCALE = HEAD_DIM ** -0.5
IDX_SCALE = IDX_DIM ** -0.5
WI_SCALE = IDX_HEADS ** -0.5
NEG_INF = float("-inf")
M_INIT = -1e30

OFF_Q, OFF_K, OFF_V, OFF_QI, OFF_KIW, OFF_U, OFF_VG = 0, 512, 640, 768, 1024, 1152, 1664
IN_COLS = 2176
KEY_TILE = 512
Q_BLOCK = LANES
Q_BLOCK_PROMPT = 2 * LANES
SEARCH_CHUNK = 256
VT_ROWS = 80
VMEM_LIMIT = 56 * 1024 * 1024


def _cparams(sem):
    return pltpu.CompilerParams(dimension_semantics=sem, vmem_limit_bytes=VMEM_LIMIT)


def _layer_norm(x, g, b):
    mu = jnp.mean(x, axis=-1, keepdims=True)
    d = x - mu
    var = jnp.mean(d * d, axis=-1, keepdims=True)
    return d * lax.rsqrt(var + LN_EPS) * g + b


def _rope(y, cos, sn, sp):
    w = y.shape[-1]
    half = ROPE_DIM // 2
    return y * cos + pltpu.roll(y, w - half, 1) * sn + pltpu.roll(y, half, 1) * sp


def _tile_lanes(a, n):
    return a if n == 1 else jnp.concatenate([a] * n, axis=1)


def _inproj_body(x_ref, w_ref, cos_ref, sn_ref, sp_ref, kig_ref, kib_ref, sg_ref, sb_ref,
                 ws_ref, bst_ref,
                 qp_ref, kf_ref, vf_ref, kb_ref, vb_ref, qia_ref, kiw_ref, kihl_ref, ob_ref,
                 *vn_refs, cl, v_transposed):
    tm = x_ref.shape[0]
    xb = x_ref[...].astype(BF16)

    def proj(lo, n):
        return jnp.dot(xb, w_ref[:, lo:lo + n], preferred_element_type=F32)

    cos, sn, sp = cos_ref[...], sn_ref[...], sp_ref[...]
    lane = lax.broadcasted_iota(jnp.int32, (tm, LANES), 1)
    left = lane < IDX_DIM

    q = _rope(proj(OFF_Q, ATTN_DIM), _tile_lanes(cos, 4), _tile_lanes(sn, 4), _tile_lanes(sp, 4)) * Q_SCALE
    heads_per_kv = N_HEADS // N_KV_HEADS
    for j in range(ATTN_DIM // LANES):
        tile = q[:, j * LANES:(j + 1) * LANES]
        rolled = pltpu.roll(tile, HEAD_DIM, 1)
        on_left = (2 * j) // heads_per_kv == 0
        keep = left if on_left else jnp.logical_not(left)
        even = jnp.where(keep, tile if on_left else rolled, 0.0)
        odd = jnp.where(keep, rolled if on_left else tile, 0.0)
        qp_ref[:, (2 * j) * LANES:(2 * j + 1) * LANES] = even.astype(BF16)
        qp_ref[:, (2 * j + 1) * LANES:(2 * j + 2) * LANES] = odd.astype(BF16)

    k = _rope(proj(OFF_K, KV_DIM), cos, sn, sp)
    kf_ref[...] = k
    kb_ref[...] = k.astype(BF16)
    v = proj(OFF_V, KV_DIM)
    vf_ref[...] = v
    vb = v.astype(BF16)
    if v_transposed:
        r_e = lax.broadcasted_iota(jnp.int32, (N_KV_HEADS * VT_ROWS, LANES), 0)
        c_e = lax.broadcasted_iota(jnp.int32, (N_KV_HEADS * VT_ROWS, LANES), 1)
        head, d = r_e // VT_ROWS, r_e % VT_ROWS
        sel = jnp.where((d < HEAD_DIM) & (c_e == head * HEAD_DIM + d), 1.0, 0.0).astype(BF16)
        vt = lax.dot_general(sel, vb, (((1,), (1,)), ((), ())), preferred_element_type=F32)
        r_o = lax.broadcasted_iota(jnp.int32, (N_KV_HEADS * VT_ROWS, tm), 0)
        vb_ref[...] = jnp.where(r_o % VT_ROWS == HEAD_DIM, 1.0, vt).astype(BF16)
    else:
        vb_ref[...] = vb

    qi = _rope(proj(OFF_QI, IDX_HEADS * IDX_DIM), _tile_lanes(cos, 2), _tile_lanes(sn, 2),
               _tile_lanes(sp, 2)) * IDX_SCALE
    qi_hi = qi.astype(BF16).astype(F32)
    qi_lo = qi - qi_hi
    for t in range(IDX_HEADS // 2):
        h_t = qi_hi[:, t * LANES:(t + 1) * LANES]
        l_t = qi_lo[:, t * LANES:(t + 1) * LANES]
        a0 = jnp.where(left, h_t, pltpu.roll(l_t, IDX_DIM, 1))
        a1 = jnp.where(left, pltpu.roll(h_t, IDX_DIM, 1), l_t)
        for j, a in enumerate((a0, a1)):
            base = (2 * t + j) * 2 * LANES
            qia_ref[:, base:base + 2 * LANES] = jnp.concatenate([a, a], axis=1).astype(BF16)

    slab = proj(OFF_KIW, LANES)
    mu = jnp.sum(jnp.where(left, slab, 0.0), axis=1, keepdims=True) * (1.0 / IDX_DIM)
    d = jnp.where(left, slab - mu, 0.0)
    var = jnp.sum(d * d, axis=1, keepdims=True) * (1.0 / IDX_DIM)
    kin = d * lax.rsqrt(var + LN_EPS) * kig_ref[...] + kib_ref[...]
    slab = jnp.where(left, kin, slab)
    slab = _rope(slab, jnp.where(left, cos, 1.0), jnp.where(left, sn, 0.0), jnp.where(left, sp, 0.0))
    kiw_ref[...] = slab
    kz = jnp.where(left, slab, 0.0)
    k_hi = kz.astype(BF16).astype(F32)
    k_lo = kz - k_hi
    hh = k_hi + pltpu.roll(k_hi, IDX_DIM, 1)
    ll = k_lo + pltpu.roll(k_lo, IDX_DIM, 1)
    kihl_ref[...] = jnp.concatenate([hh, ll], axis=1).astype(BF16)

    u = proj(OFF_U, GMLP_DIM)
    vn = _layer_norm(proj(OFF_VG, GMLP_DIM), sg_ref[...], sb_ref[...])
    if vn_refs:
        vn_refs[0][...] = vn
    vnb = vn.astype(BF16)
    r_i = lax.broadcasted_iota(jnp.int32, (cl, cl), 0)
    c_i = lax.broadcasted_iota(jnp.int32, (cl, cl), 1)
    for g in range(GMLP_GROUPS):
        wg = jnp.where(r_i >= c_i, ws_ref[g], 0.0).astype(BF16)
        bg = bst_ref[:, g:g + 1]
        gs = slice(g * GMLP_GROUP_DIM, (g + 1) * GMLP_GROUP_DIM)
        for c in range(tm // cl):
            rs = slice(c * cl, (c + 1) * cl)
            mix = jnp.dot(wg, vnb[rs, gs], preferred_element_type=F32) + bg
            ob_ref[rs, gs] = (u[rs, gs] * mix).astype(BF16)


def _layer_spec(a, layer):
    return pl.BlockSpec((None,) + a.shape[1:], lambda i: (layer,) + (0,) * (a.ndim - 1),
                        pipeline_mode=pl.Buffered(1))


def _inproj(x2d, w_a, tabs, kig, kib, sg, sb, ws, bst, *, layer, tm, cl, emit_vn, v_transposed):
    m = x2d.shape[0]
    cos, sn, sp = tabs
    nt = cos.shape[0] // tm
    row = lambda w: pl.BlockSpec((tm, w), lambda i: (i, 0))
    tab = pl.BlockSpec((tm, LANES), lambda i: (i % nt, 0))
    full = lambda a: _layer_spec(a, layer)
    if v_transposed:
        vb_shape = jax.ShapeDtypeStruct((m // tm, N_KV_HEADS * VT_ROWS, tm), BF16)
        vb_spec = pl.BlockSpec((None, N_KV_HEADS * VT_ROWS, tm), lambda i: (i, 0, 0))
    else:
        vb_shape, vb_spec = jax.ShapeDtypeStruct((m, KV_DIM), BF16), row(KV_DIM)
    out_shapes = [
        jax.ShapeDtypeStruct((m, N_HEADS * LANES), BF16),
        jax.ShapeDtypeStruct((m, KV_DIM), F32),
        jax.ShapeDtypeStruct((m, KV_DIM), F32),
        jax.ShapeDtypeStruct((m, KV_DIM), BF16),
        vb_shape,
        jax.ShapeDtypeStruct((m, IDX_HEADS * 4 * IDX_DIM), BF16),
        jax.ShapeDtypeStruct((m, LANES), F32),
        jax.ShapeDtypeStruct((m, 4 * IDX_DIM), BF16),
        jax.ShapeDtypeStruct((m, GMLP_DIM), BF16),
    ]
    out_specs = [row(N_HEADS * LANES), row(KV_DIM), row(KV_DIM), row(KV_DIM), vb_spec,
                 row(IDX_HEADS * 4 * IDX_DIM), row(LANES), row(4 * IDX_DIM), row(GMLP_DIM)]
    if emit_vn:
        out_shapes.append(jax.ShapeDtypeStruct((m, GMLP_DIM), F32))
        out_specs.append(row(GMLP_DIM))
    return pl.pallas_call(
        functools.partial(_inproj_body, cl=cl, v_transposed=v_transposed),
        grid=(m // tm,),
        in_specs=[row(D_MODEL), full(w_a), tab, tab, tab, full(kig), full(kib), full(sg), full(sb),
                  full(ws), full(bst)],
        out_specs=out_specs,
        out_shape=out_shapes,
        compiler_params=_cparams(("parallel",)),
        name="inproj",
    )(x2d, w_a, cos, sn, sp, kig, kib, sg, sb, ws, bst)


def _nt_dot(a, b):
    return lax.dot_general(a, b, (((1,), (1,)), ((), ())), preferred_element_type=F32)


def _tree_sum(parts):
    while len(parts) > 1:
        parts = [parts[j] + parts[j + 1] for j in range(0, len(parts) - 1, 2)] + (
            [parts[-1]] if len(parts) % 2 else [])
    return parts[0]


def _attn_body(wit_ref, qp_ref, qia_ref, kihl_ref, kb_ref, vt_ref, o_ref,
               keys_ref, bias_ref, s_ref, *, qb, n_valid, pos0, s_true, topk):
    i = pl.program_id(1)
    sc_n = SEARCH_CHUNK
    cpt = KEY_TILE // sc_n
    grp = N_HEADS // N_KV_HEADS
    qlane = lax.broadcasted_iota(jnp.int32, (1, qb), 1)
    qpos = pos0 + i * qb + qlane
    limit = jnp.minimum((qpos // CHUNK + 1) * CHUNK, s_true)
    kv_len = jnp.minimum(((pos0 + i * qb + qb - 1) // CHUNK + 1) * CHUNK, s_true)
    n_kt = (kv_len + KEY_TILE - 1) // KEY_TILE
    n_c = (kv_len + sc_n - 1) // sc_n
    krow = lax.broadcasted_iota(jnp.int32, (sc_n, 1), 0)

    wit = wit_ref[...] * WI_SCALE
    hw = 4 * IDX_DIM
    pairs = [jnp.concatenate([qia_ref[:, (2 * hp) * hw:(2 * hp + 1) * hw],
                              qia_ref[:, (2 * hp + 1) * hw:(2 * hp + 2) * hw]], axis=0)
             for hp in range(IDX_HEADS // 2)]

    def score_tiles(tiles):
        for t in tiles:
            kt = kihl_ref[pl.ds(pl.multiple_of(t * KEY_TILE, KEY_TILE), KEY_TILE), :]
            sc = jnp.zeros((KEY_TILE, qb), F32)
            for hp, a2 in enumerate(pairs):
                dots = jnp.maximum(_nt_dot(kt, a2), 0.0)
                sc = (sc + wit[2 * hp:2 * hp + 1, :] * dots[:, :qb]
                      + wit[2 * hp + 1:2 * hp + 2, :] * dots[:, qb:])
            for j in range(cpt):
                kidx = t * KEY_TILE + j * sc_n + krow
                s_j = jnp.where(kidx < limit, sc[j * sc_n:(j + 1) * sc_n], NEG_INF)
                s_j = jnp.where(s_j == 0.0, 0.0, s_j)
                bits = lax.bitcast_convert_type(s_j, jnp.int32)
                keys_ref[t * cpt + j] = bits ^ ((bits >> 31) & 0x7FFFFFFF)

    n_pair = n_kt // 2
    odd = n_kt % 2 == 1

    def score_pair(pr, carry):
        score_tiles((2 * pr, 2 * pr + 1))
        return carry

    lax.fori_loop(0, n_pair, score_pair, 0)
    pl.when(odd)(lambda: score_tiles((n_kt - 1,)))

    n_acc = 4

    def count(pred, thr):
        def body(c, accs):
            ones = jnp.where(pred(keys_ref[c], thr), 1.0, 0.0)
            rows = [ones[SUBLANES * j:SUBLANES * (j + 1)] for j in range(sc_n // SUBLANES)]
            per = len(rows) // n_acc
            return tuple(a + _tree_sum(rows[k * per:(k + 1) * per]) for k, a in enumerate(accs))
        accs = lax.fori_loop(0, n_c, body, tuple(jnp.zeros((SUBLANES, qb), F32) for _ in range(n_acc)))
        return jnp.sum(_tree_sum(list(accs)), axis=0, keepdims=True)

    def bit_step(it, carry):
        thr, best = carry
        cand = thr + jnp.left_shift(jnp.int32(1), 31 - it)
        cnt = count(lambda kk, t_: kk >= t_, cand)
        ok = cnt >= topk
        return jnp.where(ok, cand, thr), jnp.where(ok, cnt, best)

    thr, n_ge = lax.fori_loop(0, 32, bit_step,
                              (jnp.full((1, qb), -2 ** 31, jnp.int32), jnp.zeros((1, qb), F32)))

    real = qlane < n_valid
    overflow = jnp.max(jnp.where(real, n_ge, 0.0)) > topk

    @pl.when(jnp.logical_not(overflow))
    def _():
        def body(c, carry):
            sel = (keys_ref[c] >= thr) & (c * sc_n + krow < limit)
            bias_ref[c] = jnp.where(sel, 0.0, NEG_INF)
            return carry
        lax.fori_loop(0, n_kt * cpt, body, 0)

    @pl.when(overflow)
    def _():
        need = topk - count(lambda kk, t_: kk > t_, thr)
        r_i = lax.broadcasted_iota(jnp.int32, (sc_n, sc_n), 0)
        c_i = lax.broadcasted_iota(jnp.int32, (sc_n, sc_n), 1)
        lower = jnp.where(c_i <= r_i, 1.0, 0.0).astype(BF16)

        def tie_chunk(c, off):
            kk = keys_ref[c]
            eq = kk == thr
            local = jnp.dot(lower, jnp.where(eq, 1.0, 0.0).astype(BF16), preferred_element_type=F32)
            take = eq & (local + off <= need)
            sel = ((kk > thr) | take) & (c * sc_n + krow < limit)
            bias_ref[c] = jnp.where(sel, 0.0, NEG_INF)
            return off + local[sc_n - 1:sc_n, :]

        lax.fori_loop(0, n_kt * cpt, tie_chunk, jnp.zeros((1, qb), F32))

    r_e = lax.broadcasted_iota(jnp.int32, (LANES, LANES), 0)
    c_e = lax.broadcasted_iota(jnp.int32, (LANES, LANES), 1)
    eye = jnp.where(r_e == c_e, 1.0, 0.0).astype(BF16)
    kv_heads = range(N_KV_HEADS)
    for part in range(qb // LANES):
        q0 = part * LANES
        part_len = jnp.minimum(((pos0 + i * qb + q0 + LANES - 1) // CHUNK + 1) * CHUNK, s_true)
        part_kt = (part_len + KEY_TILE - 1) // KEY_TILE
        part_pair = part_kt // 2
        part_odd = part_kt % 2 == 1
        qns = [jnp.concatenate(
            [qp_ref[q0:q0 + LANES, (grp * n + g) * LANES:(grp * n + g + 1) * LANES] for g in range(grp)],
            axis=0) for n in kv_heads]

        def pass_a(tiles, ms, q0=q0, qns=qns):
            ms = list(ms)
            for t in tiles:
                kt = kb_ref[pl.ds(pl.multiple_of(t * KEY_TILE, KEY_TILE), KEY_TILE), :]
                bias = jnp.concatenate([bias_ref[t * cpt + j, :, q0:q0 + LANES] for j in range(cpt)], axis=0)
                bias = jnp.concatenate([bias] * grp, axis=1)
                for n in kv_heads:
                    s = _nt_dot(kt, qns[n]) + bias
                    s_ref[n, t] = s
                    ms[n] = jnp.maximum(ms[n], jnp.max(s, axis=0, keepdims=True))
            return tuple(ms)

        def pass_b(tiles, accs, ms):
            accs = list(accs)
            for t in tiles:
                for n in kv_heads:
                    p = jnp.exp(s_ref[n, t] - ms[n]).astype(BF16)
                    vt = vt_ref[t, n * VT_ROWS:(n + 1) * VT_ROWS, :]
                    accs[n] = accs[n] + jnp.dot(vt, p, preferred_element_type=F32)
            return tuple(accs)

        ms = tuple(jnp.full((1, grp * LANES), M_INIT, F32) for _ in kv_heads)
        ms = lax.fori_loop(0, part_pair, lambda pr, c: pass_a((2 * pr, 2 * pr + 1), c), ms)
        ms = lax.cond(part_odd, lambda c: pass_a((part_kt - 1,), c), lambda c: c, ms)
        accs = tuple(jnp.zeros((VT_ROWS, grp * LANES), F32) for _ in kv_heads)
        accs = lax.fori_loop(0, part_pair, lambda pr, c, ms=ms: pass_b((2 * pr, 2 * pr + 1), c, ms), accs)
        accs = lax.cond(part_odd, lambda c, ms=ms: pass_b((part_kt - 1,), c, ms), lambda c: c, accs)

        for n in kv_heads:
            acc = accs[n]
            ot = (acc[:HEAD_DIM] / acc[HEAD_DIM:HEAD_DIM + 1]).astype(BF16)
            for g in range(grp):
                h = grp * n + g
                o_ref[q0:q0 + LANES, h * HEAD_DIM:(h + 1) * HEAD_DIM] = _nt_dot(
                    eye, ot[:, g * LANES:(g + 1) * LANES]).astype(o_ref.dtype)


def _attention(wit, qp, qia, kihl, kb, vt, *, qb, n_valid, pos0, s_true, topk):
    b, t, _ = qp.shape
    s_pad = kihl.shape[1]
    assert t % qb == 0 and s_pad % KEY_TILE == 0
    grp = N_HEADS // N_KV_HEADS
    qblk = lambda w: pl.BlockSpec((None, qb, w), lambda bi, i: (bi, i, 0))
    kblk = lambda w: pl.BlockSpec((None, s_pad, w), lambda bi, i: (bi, 0, 0))
    return pl.pallas_call(
        functools.partial(_attn_body, qb=qb, n_valid=n_valid, pos0=pos0, s_true=s_true, topk=topk),
        grid=(b, t // qb),
        in_specs=[pl.BlockSpec((None, None, SUBLANES, qb), lambda bi, i: (bi, i, 0, 0)),
                  qblk(N_HEADS * LANES), qblk(IDX_HEADS * 4 * IDX_DIM),
                  kblk(4 * IDX_DIM), kblk(KV_DIM),
                  pl.BlockSpec((None, s_pad // KEY_TILE, N_KV_HEADS * VT_ROWS, KEY_TILE),
                               lambda bi, i: (bi, 0, 0, 0))],
        out_specs=qblk(ATTN_DIM),
        out_shape=jax.ShapeDtypeStruct((b, t, ATTN_DIM), BF16),
        scratch_shapes=[
            pltpu.VMEM((s_pad // SEARCH_CHUNK, SEARCH_CHUNK, qb), jnp.int32),
            pltpu.VMEM((s_pad // SEARCH_CHUNK, SEARCH_CHUNK, qb), F32),
            pltpu.VMEM((N_KV_HEADS, s_pad // KEY_TILE, KEY_TILE, grp * LANES), F32),
        ],
        compiler_params=_cparams(("parallel", "arbitrary")),
        name="sparse_attn",
    )(wit, qp, qia, kihl, kb, vt)


FF_SPLIT = 4


def _post_body(x_ref, oa_ref, ob_ref, wga_ref, wgb_ref, wpa_ref, wpb_ref, wout_ref, g1_ref, b1_ref,
               w1_ref, w2_ref, g2_ref, b2_ref, o_ref):
    x = x_ref[...]
    xb = x.astype(BF16)
    ga = jnp.dot(xb, wga_ref[...], preferred_element_type=F32)
    gb = jnp.dot(xb, wgb_ref[...], preferred_element_type=F32)
    pa = jnp.dot(oa_ref[...], wpa_ref[...], preferred_element_type=F32)
    pb = jnp.dot(ob_ref[...], wpb_ref[...], preferred_element_type=F32)
    merged = jax.nn.sigmoid(ga) * pa + jax.nn.sigmoid(gb) * pb
    h = ALPHA * x + jnp.dot(merged.astype(BF16), wout_ref[...], preferred_element_type=F32)
    x1 = _layer_norm(h, g1_ref[...], b1_ref[...])
    x1b = x1.astype(BF16)
    cw = D_FF // FF_SPLIT
    ff = jnp.zeros(x.shape, F32)
    for c in range(FF_SPLIT):
        hcol = jnp.maximum(jnp.dot(x1b, w1_ref[:, c * cw:(c + 1) * cw], preferred_element_type=F32), 0.0)
        ff = ff + jnp.dot((hcol * hcol).astype(BF16), w2_ref[c * cw:(c + 1) * cw, :],
                          preferred_element_type=F32)
    o_ref[...] = _layer_norm(ALPHA * x1 + ff, g2_ref[...], b2_ref[...])


def _post_block(x2d, oa, ob, wga, wgb, wpa, wpb, wout, g1, b1, w1, w2, g2, b2, *, layer, tm):
    m = x2d.shape[0]
    row = lambda w: pl.BlockSpec((tm, w), lambda i: (i, 0))
    full = lambda a: _layer_spec(a, layer)
    consts = (wga, wgb, wpa, wpb, wout, g1, b1, w1, w2, g2, b2)
    return pl.pallas_call(
        _post_body,
        grid=(m // tm,),
        in_specs=[row(D_MODEL), row(ATTN_DIM), row(GMLP_DIM)] + [full(a) for a in consts],
        out_specs=row(D_MODEL),
        out_shape=jax.ShapeDtypeStruct((m, D_MODEL), F32),
        compiler_params=_cparams(("parallel",)),
        name="post_block",
    )(x2d, oa, ob, *consts)


def _rope_tables(pos, rows):
    half = ROPE_DIM // 2
    freqs = ROPE_THETA ** (-jnp.arange(half, dtype=F32) * 2.0 / ROPE_DIM)
    ang = pos.astype(F32)[:, None] * freqs[None, :]
    cos, sin = jnp.cos(ang), jnp.sin(ang)
    t = pos.shape[0]
    rest = HEAD_DIM - ROPE_DIM
    cos_h = jnp.concatenate([cos, cos, jnp.ones((t, rest), F32)], axis=1)
    sn_h = jnp.concatenate([-sin, jnp.zeros((t, half + rest), F32)], axis=1)
    sp_h = jnp.concatenate([jnp.zeros((t, half), F32), sin, jnp.zeros((t, rest), F32)], axis=1)
    reps = (max(rows // t, 1), LANES // HEAD_DIM)
    return tuple(jnp.tile(a, reps) for a in (cos_h, sn_h, sp_h))


def _split_w_in(w):
    wb = w.astype(BF16)
    n_head = OFF_KIW + IDX_DIM + IDX_HEADS
    col = lax.broadcasted_iota(jnp.int32, (1, 1, OFF_U), 2)
    head = jnp.where(col < n_head, wb[..., :OFF_U], jnp.zeros((), BF16))
    tail = wb[..., n_head:]
    w_a = jnp.concatenate([head, tail[..., :2 * GMLP_DIM]], axis=-1)
    return w_a, tail[..., 2 * GMLP_DIM:2 * GMLP_DIM + D_MODEL], tail[..., 2 * GMLP_DIM + D_MODEL:]


def _split_hi_lo(ki):
    hi = ki.astype(BF16)
    lo = (ki - hi.astype(F32)).astype(BF16)
    return jnp.concatenate([hi, hi, lo, lo], axis=-1)


def _head_weights_t(kiw, qb):
    b, t, _ = kiw.shape
    wi = kiw[:, :, IDX_DIM:IDX_DIM + IDX_HEADS].reshape(b, t // qb, qb, IDX_HEADS)
    return jnp.pad(wi.transpose(0, 1, 3, 2), ((0, 0), (0, 0), (0, SUBLANES - IDX_HEADS), (0, 0)))


def _augment_vt(vt):
    b, nt, _, kt = vt.shape
    v4 = vt.reshape(b, nt, N_KV_HEADS, HEAD_DIM, kt)
    ones = jnp.ones((b, nt, N_KV_HEADS, 1, kt), vt.dtype)
    zeros = jnp.zeros((b, nt, N_KV_HEADS, VT_ROWS - HEAD_DIM - 1, kt), vt.dtype)
    return jnp.concatenate([v4, ones, zeros], axis=3).reshape(b, nt, N_KV_HEADS * VT_ROWS, kt)


def _pad_keys(a, s_pad):
    return jnp.pad(a, ((0, 0), (0, s_pad - a.shape[1]), (0, 0)))


def kernel(x_prompt, x_sample, cache_k, cache_v, cache_idx_k, w_in, idx_k_g, idx_k_b, sgu_ln_g, sgu_ln_b,
           w_s, b_s, w_pa, w_pb, w_out, ln1_g, ln1_b, w_ff1, w_ff2, ln2_g, ln2_b):
    bp, tp, _ = x_prompt.shape
    bs, ts, _ = x_sample.shape
    depth = w_in.shape[0]
    past = cache_k.shape[2]
    topk_p = min(TOPK_MAX, tp // 4)
    topk_s = min(TOPK_MAX, (past + ts) // 4)
    tm_p = KEY_TILE
    tm_s = bs * ts
    assert (bp * tp) % tm_p == 0 and tp % KEY_TILE == 0
    s_all = past + ts
    s_pad = -(-s_all // KEY_TILE) * KEY_TILE
    qb_p = Q_BLOCK_PROMPT if tp % Q_BLOCK_PROMPT == 0 else Q_BLOCK

    tabs_p = _rope_tables(jnp.arange(tp), tm_p)
    tabs_s = _rope_tables(past + jnp.arange(ts), tm_s)

    xp = x_prompt.reshape(bp * tp, D_MODEL)
    xs = x_sample.reshape(bs * ts, D_MODEL)
    pk, pv, pik, sk, sv, sik, ssv = [], [], [], [], [], [], []
    row3 = lambda a: a.reshape(depth, 1, -1)
    w_a, w_ga, w_gb = _split_w_in(w_in)
    kig = row3(jnp.pad(idx_k_g, ((0, 0), (0, LANES - IDX_DIM))))
    kib = row3(jnp.pad(idx_k_b, ((0, 0), (0, LANES - IDX_DIM))))
    sg, sb = row3(sgu_ln_g), row3(sgu_ln_b)
    wpa, wpb, wout = w_pa.astype(BF16), w_pb.astype(BF16), w_out.astype(BF16)
    w1, w2 = w_ff1.astype(BF16), w_ff2.astype(BF16)
    g1, b1, g2, b2 = row3(ln1_g), row3(ln1_b), row3(ln2_g), row3(ln2_b)
    bst_p = b_s.transpose(0, 2, 1)
    ws_s, bst_s = w_s[:, :, :ts, :ts], b_s[:, :, :ts].transpose(0, 2, 1)
    for l in range(depth):
        qp, kf, vf, kb, vt, qia, kiw, kihl, ob = _inproj(
            xp, w_a, tabs_p, kig, kib, sg, sb, w_s, bst_p, layer=l, tm=tm_p, cl=GMLP_CHUNK, emit_vn=False,
            v_transposed=True)
        r3 = lambda a: a.reshape(bp, tp, a.shape[-1])
        oa = _attention(_head_weights_t(r3(kiw), qb_p), r3(qp), r3(qia), r3(kihl), r3(kb),
                        vt.reshape(bp, tp // KEY_TILE, N_KV_HEADS * VT_ROWS, KEY_TILE),
                        qb=qb_p, n_valid=qb_p, pos0=0, s_true=tp, topk=topk_p)
        xp = _post_block(xp, oa.reshape(bp * tp, ATTN_DIM), ob, w_ga, w_gb, wpa, wpb, wout, g1, b1,
                         w1, w2, g2, b2, layer=l, tm=tm_p)
        pk.append(kf.reshape(bp, tp, N_KV_HEADS, HEAD_DIM))
        pv.append(vf.reshape(bp, tp, N_KV_HEADS, HEAD_DIM))
        pik.append(kiw[:, :IDX_DIM].reshape(bp, tp, IDX_DIM))

        qp, kf, vf, kb, vb, qia, kiw, kihl, ob, vn = _inproj(
            xs, w_a, tabs_s, kig, kib, sg, sb, ws_s, bst_s, layer=l,
            tm=tm_s, cl=ts, emit_vn=True, v_transposed=False)
        r3 = lambda a: a.reshape(bs, ts, a.shape[-1])
        k_all = _pad_keys(jnp.concatenate([cache_k[l].reshape(bs, past, KV_DIM).astype(BF16), r3(kb)], 1), s_pad)
        v_all = _pad_keys(jnp.concatenate([cache_v[l].reshape(bs, past, KV_DIM).astype(BF16), r3(vb)], 1), s_pad)
        vt_all = _augment_vt(v_all.reshape(bs, s_pad // KEY_TILE, KEY_TILE, KV_DIM).transpose(0, 1, 3, 2))
        ki_all = _pad_keys(jnp.concatenate([_split_hi_lo(cache_idx_k[l]), r3(kihl)], 1), s_pad)
        padq = lambda a: jnp.pad(r3(a), ((0, 0), (0, Q_BLOCK - ts), (0, 0)))
        oa = _attention(_head_weights_t(padq(kiw), Q_BLOCK), padq(qp), padq(qia), ki_all, k_all, vt_all,
                        qb=Q_BLOCK, n_valid=ts, pos0=past, s_true=s_all, topk=topk_s)[:, :ts]
        xs = _post_block(xs, oa.reshape(bs * ts, ATTN_DIM), ob, w_ga, w_gb, wpa, wpb, wout, g1, b1,
                         w1, w2, g2, b2, layer=l, tm=tm_s)
        sk.append(kf.reshape(bs, ts, N_KV_HEADS, HEAD_DIM))
        sv.append(vf.reshape(bs, ts, N_KV_HEADS, HEAD_DIM))
        sik.append(kiw[:, :IDX_DIM].reshape(bs, ts, IDX_DIM))
        ssv.append(vn.reshape(bs, ts, GMLP_DIM))

    return (xp.reshape(bp, tp, D_MODEL), xs.reshape(bs, ts, D_MODEL),
            jnp.stack(pk), jnp.stack(pv), jnp.stack(pik),
            jnp.stack(sk), jnp.stack(sv), jnp.stack(sik), jnp.stack(ssv))
```

```python
import functools

import jax
import jax.numpy as jnp
from jax import lax
from jax.experimental import pallas as pl
from jax.experimental.pallas import tpu as pltpu

F32 = jnp.float32
BF16 = jnp.bfloat16

D_MODEL = 1024
N_HEADS = 8
N_KV_HEADS = 2
HEAD_DIM = 64
ATTN_DIM = N_HEADS * HEAD_DIM
KV_DIM = N_KV_HEADS * HEAD_DIM
IDX_HEADS = 4
IDX_DIM = 64
TOPK_MAX = 256
CHUNK = 64
GMLP_CHUNK = 128
GMLP_GROUPS = 4
GMLP_DIM = 512
GMLP_GROUP_DIM = GMLP_DIM // GMLP_GROUPS
D_FF = 4 * D_MODEL
ROPE_THETA = 500000.0
ROPE_DIM = HEAD_DIM // 4
DEPTH = 2
ALPHA = (2 * DEPTH) ** 0.25
LN_EPS = 1e-5

LANES = 128
SUBLANES = 8
Q_SCALE = HEAD_DIM ** -0.5
IDX_SCALE = IDX_DIM ** -0.5
WI_SCALE = IDX_HEADS ** -0.5
NEG_INF = float("-inf")
M_INIT = -1e30

OFF_Q, OFF_K, OFF_V, OFF_QI, OFF_KIW, OFF_U, OFF_VG = 0, 512, 640, 768, 1024, 1152, 1664
KEY_TILE = 512
Q_BLOCK = LANES
Q_BLOCK_PROMPT = 2 * LANES
SEARCH_CHUNK = 256
VT_ROWS = 80
VMEM_LIMIT = 56 * 1024 * 1024


def _cparams(sem):
    return pltpu.CompilerParams(dimension_semantics=sem, vmem_limit_bytes=VMEM_LIMIT)


def _layer_norm(x, g, b):
    mu = jnp.mean(x, axis=-1, keepdims=True)
    d = x - mu
    var = jnp.mean(d * d, axis=-1, keepdims=True)
    return d * lax.rsqrt(var + LN_EPS) * g + b


def _rope(y, cos, sn, sp):
    w = y.shape[-1]
    half = ROPE_DIM // 2
    return y * cos + pltpu.roll(y, w - half, 1) * sn + pltpu.roll(y, half, 1) * sp


def _tile_lanes(a, n):
    return a if n == 1 else jnp.concatenate([a] * n, axis=1)


def _inproj_body(x_ref, w_ref, cos_ref, sn_ref, sp_ref, kig_ref, kib_ref, sg_ref, sb_ref,
                 ws_ref, bst_ref,
                 qp_ref, kf_ref, vf_ref, kb_ref, vb_ref, qia_ref, kiw_ref, kihl_ref, ob_ref,
                 *vn_refs, cl, v_transposed):
    tm = x_ref.shape[0]
    xb = x_ref[...].astype(BF16)

    def proj(lo, n):
        return jnp.dot(xb, w_ref[:, lo:lo + n], preferred_element_type=F32)

    cos, sn, sp = cos_ref[...], sn_ref[...], sp_ref[...]
    lane = lax.broadcasted_iota(jnp.int32, (tm, LANES), 1)
    left = lane < IDX_DIM

    q = _rope(proj(OFF_Q, ATTN_DIM), _tile_lanes(cos, 4), _tile_lanes(sn, 4), _tile_lanes(sp, 4)) * Q_SCALE
    heads_per_kv = N_HEADS // N_KV_HEADS
    for j in range(ATTN_DIM // LANES):
        tile = q[:, j * LANES:(j + 1) * LANES]
        rolled = pltpu.roll(tile, HEAD_DIM, 1)
        on_left = (2 * j) // heads_per_kv == 0
        keep = left if on_left else jnp.logical_not(left)
        even = jnp.where(keep, tile if on_left else rolled, 0.0)
        odd = jnp.where(keep, rolled if on_left else tile, 0.0)
        qp_ref[:, (2 * j) * LANES:(2 * j + 1) * LANES] = even.astype(BF16)
        qp_ref[:, (2 * j + 1) * LANES:(2 * j + 2) * LANES] = odd.astype(BF16)

    k = _rope(proj(OFF_K, KV_DIM), cos, sn, sp)
    kf_ref[...] = k
    kb_ref[...] = k.astype(BF16)
    v = proj(OFF_V, KV_DIM)
    vf_ref[...] = v
    vb = v.astype(BF16)
    if v_transposed:
        r_e = lax.broadcasted_iota(jnp.int32, (N_KV_HEADS * VT_ROWS, LANES), 0)
        c_e = lax.broadcasted_iota(jnp.int32, (N_KV_HEADS * VT_ROWS, LANES), 1)
        head, d = r_e // VT_ROWS, r_e % VT_ROWS
        sel = jnp.where((d < HEAD_DIM) & (c_e == head * HEAD_DIM + d), 1.0, 0.0).astype(BF16)
        vt = lax.dot_general(sel, vb, (((1,), (1,)), ((), ())), preferred_element_type=F32)
        r_o = lax.broadcasted_iota(jnp.int32, (N_KV_HEADS * VT_ROWS, tm), 0)
        vb_ref[...] = jnp.where(r_o % VT_ROWS == HEAD_DIM, 1.0, vt).astype(BF16)
    else:
        vb_ref[...] = vb

    qi = _rope(proj(OFF_QI, IDX_HEADS * IDX_DIM), _tile_lanes(cos, 2), _tile_lanes(sn, 2),
               _tile_lanes(sp, 2)) * IDX_SCALE
    qi_hi = qi.astype(BF16).astype(F32)
    qi_lo = qi - qi_hi
    for t in range(IDX_HEADS // 2):
        h_t = qi_hi[:, t * LANES:(t + 1) * LANES]
        l_t = qi_lo[:, t * LANES:(t + 1) * LANES]
        a0 = jnp.where(left, h_t, pltpu.roll(l_t, IDX_DIM, 1))
        a1 = jnp.where(left, pltpu.roll(h_t, IDX_DIM, 1), l_t)
        for j, a in enumerate((a0, a1)):
            base = (2 * t + j) * 2 * LANES
            qia_ref[:, base:base + 2 * LANES] = jnp.concatenate([a, a], axis=1).astype(BF16)

    slab = proj(OFF_KIW, LANES)
    mu = jnp.sum(jnp.where(left, slab, 0.0), axis=1, keepdims=True) * (1.0 / IDX_DIM)
    d = jnp.where(left, slab - mu, 0.0)
    var = jnp.sum(d * d, axis=1, keepdims=True) * (1.0 / IDX_DIM)
    kin = d * lax.rsqrt(var + LN_EPS) * kig_ref[...] + kib_ref[...]
    slab = jnp.where(left, kin, slab)
    slab = _rope(slab, jnp.where(left, cos, 1.0), jnp.where(left, sn, 0.0), jnp.where(left, sp, 0.0))
    kiw_ref[...] = slab
    kz = jnp.where(left, slab, 0.0)
    k_hi = kz.astype(BF16).astype(F32)
    k_lo = kz - k_hi
    hh = k_hi + pltpu.roll(k_hi, IDX_DIM, 1)
    ll = k_lo + pltpu.roll(k_lo, IDX_DIM, 1)
    kihl_ref[...] = jnp.concatenate([hh, ll], axis=1).astype(BF16)

    u = proj(OFF_U, GMLP_DIM)
    vn = _layer_norm(proj(OFF_VG, GMLP_DIM), sg_ref[...], sb_ref[...])
    if vn_refs:
        vn_refs[0][...] = vn
    vnb = vn.astype(BF16)
    r_i = lax.broadcasted_iota(jnp.int32, (cl, cl), 0)
    c_i = lax.broadcasted_iota(jnp.int32, (cl, cl), 1)
    for g in range(GMLP_GROUPS):
        wg = jnp.where(r_i >= c_i, ws_ref[g], 0.0).astype(BF16)
        bg = bst_ref[:, g:g + 1]
        gs = slice(g * GMLP_GROUP_DIM, (g + 1) * GMLP_GROUP_DIM)
        for c in range(tm // cl):
            rs = slice(c * cl, (c + 1) * cl)
            mix = jnp.dot(wg, vnb[rs, gs], preferred_element_type=F32) + bg
            ob_ref[rs, gs] = (u[rs, gs] * mix).astype(BF16)


def _layer_spec(a, layer):
    return pl.BlockSpec((None,) + a.shape[1:], lambda i: (layer,) + (0,) * (a.ndim - 1),
                        pipeline_mode=pl.Buffered(1))


def _inproj(x2d, w_a, tabs, kig, kib, sg, sb, ws, bst, *, layer, tm, cl, emit_vn, v_transposed):
    m = x2d.shape[0]
    cos, sn, sp = tabs
    nt = cos.shape[0] // tm
    row = lambda w: pl.BlockSpec((tm, w), lambda i: (i, 0))
    tab = pl.BlockSpec((tm, LANES), lambda i: (i % nt, 0))
    full = lambda a: _layer_spec(a, layer)
    if v_transposed:
        vb_shape = jax.ShapeDtypeStruct((m // tm, N_KV_HEADS * VT_ROWS, tm), BF16)
        vb_spec = pl.BlockSpec((None, N_KV_HEADS * VT_ROWS, tm), lambda i: (i, 0, 0))
    else:
        vb_shape, vb_spec = jax.ShapeDtypeStruct((m, KV_DIM), BF16), row(KV_DIM)
    out_shapes = [
        jax.ShapeDtypeStruct((m, N_HEADS * LANES), BF16),
        jax.ShapeDtypeStruct((m, KV_DIM), F32),
        jax.ShapeDtypeStruct((m, KV_DIM), F32),
        jax.ShapeDtypeStruct((m, KV_DIM), BF16),
        vb_shape,
        jax.ShapeDtypeStruct((m, IDX_HEADS * 4 * IDX_DIM), BF16),
        jax.ShapeDtypeStruct((m, LANES), F32),
        jax.ShapeDtypeStruct((m, 4 * IDX_DIM), BF16),
        jax.ShapeDtypeStruct((m, GMLP_DIM), BF16),
    ]
    out_specs = [row(N_HEADS * LANES), row(KV_DIM), row(KV_DIM), row(KV_DIM), vb_spec,
                 row(IDX_HEADS * 4 * IDX_DIM), row(LANES), row(4 * IDX_DIM), row(GMLP_DIM)]
    if emit_vn:
        out_shapes.append(jax.ShapeDtypeStruct((m, GMLP_DIM), F32))
        out_specs.append(row(GMLP_DIM))
    return pl.pallas_call(
        functools.partial(_inproj_body, cl=cl, v_transposed=v_transposed),
        grid=(m // tm,),
        in_specs=[row(D_MODEL), full(w_a), tab, tab, tab, full(kig), full(kib), full(sg), full(sb),
                  full(ws), full(bst)],
        out_specs=out_specs,
        out_shape=out_shapes,
        compiler_params=_cparams(("parallel",)),
        name="inproj",
    )(x2d, w_a, cos, sn, sp, kig, kib, sg, sb, ws, bst)


def _nt_dot(a, b):
    return lax.dot_general(a, b, (((1,), (1,)), ((), ())), preferred_element_type=F32)


def _tree_sum(parts):
    while len(parts) > 1:
        parts = [parts[j] + parts[j + 1] for j in range(0, len(parts) - 1, 2)] + (
            [parts[-1]] if len(parts) % 2 else [])
    return parts[0]


def _attn_body(wit_ref, qp_ref, qia_ref, kihl_ref, kb_ref, vt_ref, o_ref,
               keys_ref, bias_ref, s_ref, *, qb, n_valid, pos0, s_true, topk):
    i = pl.program_id(1)
    sc_n = SEARCH_CHUNK
    cpt = KEY_TILE // sc_n
    grp = N_HEADS // N_KV_HEADS
    qlane = lax.broadcasted_iota(jnp.int32, (1, qb), 1)
    qpos = pos0 + i * qb + qlane
    limit = jnp.minimum((qpos // CHUNK + 1) * CHUNK, s_true)
    kv_len = jnp.minimum(((pos0 + i * qb + qb - 1) // CHUNK + 1) * CHUNK, s_true)
    n_kt = (kv_len + KEY_TILE - 1) // KEY_TILE
    n_c = (kv_len + sc_n - 1) // sc_n
    krow = lax.broadcasted_iota(jnp.int32, (sc_n, 1), 0)

    wit = wit_ref[...] * WI_SCALE
    hw = 4 * IDX_DIM
    pairs = [jnp.concatenate([qia_ref[:, (2 * hp) * hw:(2 * hp + 1) * hw],
                              qia_ref[:, (2 * hp + 1) * hw:(2 * hp + 2) * hw]], axis=0)
             for hp in range(IDX_HEADS // 2)]

    def score_tiles(tiles):
        for t in tiles:
            kt = kihl_ref[pl.ds(pl.multiple_of(t * KEY_TILE, KEY_TILE), KEY_TILE), :]
            sc = jnp.zeros((KEY_TILE, qb), F32)
            for hp, a2 in enumerate(pairs):
                dots = jnp.maximum(_nt_dot(kt, a2), 0.0)
                sc = (sc + wit[2 * hp:2 * hp + 1, :] * dots[:, :qb]
                      + wit[2 * hp + 1:2 * hp + 2, :] * dots[:, qb:])
            for j in range(cpt):
                kidx = t * KEY_TILE + j * sc_n + krow
                s_j = jnp.where(kidx < limit, sc[j * sc_n:(j + 1) * sc_n], NEG_INF)
                s_j = jnp.where(s_j == 0.0, 0.0, s_j)
                bits = lax.bitcast_convert_type(s_j, jnp.int32)
                keys_ref[t * cpt + j] = bits ^ ((bits >> 31) & 0x7FFFFFFF)

    n_pair = n_kt // 2
    odd = n_kt % 2 == 1

    def score_pair(pr, carry):
        score_tiles((2 * pr, 2 * pr + 1))
        return carry

    lax.fori_loop(0, n_pair, score_pair, 0)
    pl.when(odd)(lambda: score_tiles((n_kt - 1,)))

    n_acc = 4

    def count(pred, thr):
        def body(c, accs):
            ones = jnp.where(pred(keys_ref[c], thr), 1.0, 0.0)
            rows = [ones[SUBLANES * j:SUBLANES * (j + 1)] for j in range(sc_n // SUBLANES)]
            per = len(rows) // n_acc
            return tuple(a + _tree_sum(rows[k * per:(k + 1) * per]) for k, a in enumerate(accs))
        accs = lax.fori_loop(0, n_c, body, tuple(jnp.zeros((SUBLANES, qb), F32) for _ in range(n_acc)))
        return jnp.sum(_tree_sum(list(accs)), axis=0, keepdims=True)

    def bit_step(it, carry):
        thr, best = carry
        cand = thr + jnp.left_shift(jnp.int32(1), 31 - it)
        cnt = count(lambda kk, t_: kk >= t_, cand)
        ok = cnt >= topk
        return jnp.where(ok, cand, thr), jnp.where(ok, cnt, best)

    thr, n_ge = lax.fori_loop(0, 32, bit_step,
                              (jnp.full((1, qb), -2 ** 31, jnp.int32), jnp.zeros((1, qb), F32)))

    real = qlane < n_valid
    overflow = jnp.max(jnp.where(real, n_ge, 0.0)) > topk

    @pl.when(jnp.logical_not(overflow))
    def _():
        def body(c, carry):
            sel = (keys_ref[c] >= thr) & (c * sc_n + krow < limit)
            bias_ref[c] = jnp.where(sel, 0.0, NEG_INF)
            return carry
        lax.fori_loop(0, n_kt * cpt, body, 0)

    @pl.when(overflow)
    def _():
        need = topk - count(lambda kk, t_: kk > t_, thr)
        r_i = lax.broadcasted_iota(jnp.int32, (sc_n, sc_n), 0)
        c_i = lax.broadcasted_iota(jnp.int32, (sc_n, sc_n), 1)
        lower = jnp.where(c_i <= r_i, 1.0, 0.0).astype(BF16)

        def tie_chunk(c, off):
            kk = keys_ref[c]
            eq = kk == thr
            local = jnp.dot(lower, jnp.where(eq, 1.0, 0.0).astype(BF16), preferred_element_type=F32)
            take = eq & (local + off <= need)
            sel = ((kk > thr) | take) & (c * sc_n + krow < limit)
            bias_ref[c] = jnp.where(sel, 0.0, NEG_INF)
            return off + local[sc_n - 1:sc_n, :]

        lax.fori_loop(0, n_kt * cpt, tie_chunk, jnp.zeros((1, qb), F32))

    r_e = lax.broadcasted_iota(jnp.int32, (LANES, LANES), 0)
    c_e = lax.broadcasted_iota(jnp.int32, (LANES, LANES), 1)
    eye = jnp.where(r_e == c_e, 1.0, 0.0).astype(BF16)
    kv_heads = range(N_KV_HEADS)
    for part in range(qb // LANES):
        q0 = part * LANES
        part_len = jnp.minimum(((pos0 + i * qb + q0 + LANES - 1) // CHUNK + 1) * CHUNK, s_true)
        part_kt = (part_len + KEY_TILE - 1) // KEY_TILE
        part_pair = part_kt // 2
        part_odd = part_kt % 2 == 1
        qns = [jnp.concatenate(
            [qp_ref[q0:q0 + LANES, (grp * n + g) * LANES:(grp * n + g + 1) * LANES] for g in range(grp)],
            axis=0) for n in kv_heads]

        def pass_a(tiles, ms, q0=q0, qns=qns):
            ms = list(ms)
            for t in tiles:
                kt = kb_ref[pl.ds(pl.multiple_of(t * KEY_TILE, KEY_TILE), KEY_TILE), :]
                bias = jnp.concatenate([bias_ref[t * cpt + j, :, q0:q0 + LANES] for j in range(cpt)], axis=0)
                bias = jnp.concatenate([bias] * grp, axis=1)
                for n in kv_heads:
                    s = _nt_dot(kt, qns[n]) + bias
                    s_ref[n, t] = s
                    ms[n] = jnp.maximum(ms[n], jnp.max(s, axis=0, keepdims=True))
            return tuple(ms)

        def pass_b(tiles, accs, ms):
            accs = list(accs)
            for t in tiles:
                for n in kv_heads:
                    p = jnp.exp(s_ref[n, t] - ms[n]).astype(BF16)
                    vt = vt_ref[t, n * VT_ROWS:(n + 1) * VT_ROWS, :]
                    accs[n] = accs[n] + jnp.dot(vt, p, preferred_element_type=F32)
            return tuple(accs)

        ms = tuple(jnp.full((1, grp * LANES), M_INIT, F32) for _ in kv_heads)
        ms = lax.fori_loop(0, part_pair, lambda pr, c: pass_a((2 * pr, 2 * pr + 1), c), ms)
        ms = lax.cond(part_odd, lambda c: pass_a((part_kt - 1,), c), lambda c: c, ms)
        accs = tuple(jnp.zeros((VT_ROWS, grp * LANES), F32) for _ in kv_heads)
        accs = lax.fori_loop(0, part_pair, lambda pr, c, ms=ms: pass_b((2 * pr, 2 * pr + 1), c, ms), accs)
        accs = lax.cond(part_odd, lambda c, ms=ms: pass_b((part_kt - 1,), c, ms), lambda c: c, accs)

        for n in kv_heads:
            acc = accs[n]
            ot = (acc[:HEAD_DIM] / acc[HEAD_DIM:HEAD_DIM + 1]).astype(BF16)
            for g in range(grp):
                h = grp * n + g
                o_ref[q0:q0 + LANES, h * HEAD_DIM:(h + 1) * HEAD_DIM] = _nt_dot(
                    eye, ot[:, g * LANES:(g + 1) * LANES]).astype(o_ref.dtype)


def _attention(wit, qp, qia, kihl, kb, vt, *, qb, n_valid, pos0, s_true, topk):
    b, t, _ = qp.shape
    s_pad = kihl.shape[1]
    assert t % qb == 0 and s_pad % KEY_TILE == 0
    grp = N_HEADS // N_KV_HEADS
    qblk = lambda w: pl.BlockSpec((None, qb, w), lambda bi, i: (bi, i, 0))
    kblk = lambda w: pl.BlockSpec((None, s_pad, w), lambda bi, i: (bi, 0, 0))
    return pl.pallas_call(
        functools.partial(_attn_body, qb=qb, n_valid=n_valid, pos0=pos0, s_true=s_true, topk=topk),
        grid=(b, t // qb),
        in_specs=[pl.BlockSpec((None, None, SUBLANES, qb), lambda bi, i: (bi, i, 0, 0)),
                  qblk(N_HEADS * LANES), qblk(IDX_HEADS * 4 * IDX_DIM),
                  kblk(4 * IDX_DIM), kblk(KV_DIM),
                  pl.BlockSpec((None, s_pad // KEY_TILE, N_KV_HEADS * VT_ROWS, KEY_TILE),
                               lambda bi, i: (bi, 0, 0, 0))],
        out_specs=qblk(ATTN_DIM),
        out_shape=jax.ShapeDtypeStruct((b, t, ATTN_DIM), BF16),
        scratch_shapes=[
            pltpu.VMEM((s_pad // SEARCH_CHUNK, SEARCH_CHUNK, qb), jnp.int32),
            pltpu.VMEM((s_pad // SEARCH_CHUNK, SEARCH_CHUNK, qb), F32),
            pltpu.VMEM((N_KV_HEADS, s_pad // KEY_TILE, KEY_TILE, grp * LANES), F32),
        ],
        compiler_params=_cparams(("parallel", "arbitrary")),
        name="sparse_attn",
    )(wit, qp, qia, kihl, kb, vt)


FF_SPLIT = 4


def _post_body(x_ref, oa_ref, ob_ref, wga_ref, wgb_ref, wpa_ref, wpb_ref, wout_ref, g1_ref, b1_ref,
               w1_ref, w2_ref, g2_ref, b2_ref, o_ref):
    x = x_ref[...]
    xb = x.astype(BF16)
    ga = jnp.dot(xb, wga_ref[...], preferred_element_type=F32)
    gb = jnp.dot(xb, wgb_ref[...], preferred_element_type=F32)
    pa = jnp.dot(oa_ref[...], wpa_ref[...], preferred_element_type=F32)
    pb = jnp.dot(ob_ref[...], wpb_ref[...], preferred_element_type=F32)
    merged = jax.nn.sigmoid(ga) * pa + jax.nn.sigmoid(gb) * pb
    h = ALPHA * x + jnp.dot(merged.astype(BF16), wout_ref[...], preferred_element_type=F32)
    x1 = _layer_norm(h, g1_ref[...], b1_ref[...])
    x1b = x1.astype(BF16)
    cw = D_FF // FF_SPLIT
    ff = jnp.zeros(x.shape, F32)
    for c in range(FF_SPLIT):
        hcol = jnp.maximum(jnp.dot(x1b, w1_ref[:, c * cw:(c + 1) * cw], preferred_element_type=F32), 0.0)
        ff = ff + jnp.dot((hcol * hcol).astype(BF16), w2_ref[c * cw:(c + 1) * cw, :],
                          preferred_element_type=F32)
    o_ref[...] = _layer_norm(ALPHA * x1 + ff, g2_ref[...], b2_ref[...])


def _post_block(x2d, oa, ob, wga, wgb, wpa, wpb, wout, g1, b1, w1, w2, g2, b2, *, layer, tm):
    m = x2d.shape[0]
    row = lambda w: pl.BlockSpec((tm, w), lambda i: (i, 0))
    full = lambda a: _layer_spec(a, layer)
    consts = (wga, wgb, wpa, wpb, wout, g1, b1, w1, w2, g2, b2)
    return pl.pallas_call(
        _post_body,
        grid=(m // tm,),
        in_specs=[row(D_MODEL), row(ATTN_DIM), row(GMLP_DIM)] + [full(a) for a in consts],
        out_specs=row(D_MODEL),
        out_shape=jax.ShapeDtypeStruct((m, D_MODEL), F32),
        compiler_params=_cparams(("parallel",)),
        name="post_block",
    )(x2d, oa, ob, *consts)


def _rope_tables(pos, rows):
    half = ROPE_DIM // 2
    freqs = ROPE_THETA ** (-jnp.arange(half, dtype=F32) * 2.0 / ROPE_DIM)
    ang = pos.astype(F32)[:, None] * freqs[None, :]
    cos, sin = jnp.cos(ang), jnp.sin(ang)
    t = pos.shape[0]
    rest = HEAD_DIM - ROPE_DIM
    cos_h = jnp.concatenate([cos, cos, jnp.ones((t, rest), F32)], axis=1)
    sn_h = jnp.concatenate([-sin, jnp.zeros((t, half + rest), F32)], axis=1)
    sp_h = jnp.concatenate([jnp.zeros((t, half), F32), sin, jnp.zeros((t, rest), F32)], axis=1)
    reps = (max(rows // t, 1), LANES // HEAD_DIM)
    return tuple(jnp.tile(a, reps) for a in (cos_h, sn_h, sp_h))


def _split_w_in(w):
    wb = w.astype(BF16)
    n_head = OFF_KIW + IDX_DIM + IDX_HEADS
    col = lax.broadcasted_iota(jnp.int32, (1, 1, OFF_U), 2)
    head = jnp.where(col < n_head, wb[..., :OFF_U], jnp.zeros((), BF16))
    tail = wb[..., n_head:]
    w_a = jnp.concatenate([head, tail[..., :2 * GMLP_DIM]], axis=-1)
    return w_a, tail[..., 2 * GMLP_DIM:2 * GMLP_DIM + D_MODEL], tail[..., 2 * GMLP_DIM + D_MODEL:]


def _split_hi_lo(ki):
    hi = ki.astype(BF16)
    lo = (ki - hi.astype(F32)).astype(BF16)
    return jnp.concatenate([hi, hi, lo, lo], axis=-1)


def _head_weights_t(kiw, qb):
    b, t, _ = kiw.shape
    wi = kiw[:, :, IDX_DIM:IDX_DIM + IDX_HEADS].reshape(b, t // qb, qb, IDX_HEADS)
    return jnp.pad(wi.transpose(0, 1, 3, 2), ((0, 0), (0, 0), (0, SUBLANES - IDX_HEADS), (0, 0)))


def _augment_vt(vt):
    b, nt, _, kt = vt.shape
    v4 = vt.reshape(b, nt, N_KV_HEADS, HEAD_DIM, kt)
    ones = jnp.ones((b, nt, N_KV_HEADS, 1, kt), vt.dtype)
    zeros = jnp.zeros((b, nt, N_KV_HEADS, VT_ROWS - HEAD_DIM - 1, kt), vt.dtype)
    return jnp.concatenate([v4, ones, zeros], axis=3).reshape(b, nt, N_KV_HEADS * VT_ROWS, kt)


def _pad_keys(a, s_pad):
    return jnp.pad(a, ((0, 0), (0, s_pad - a.shape[1]), (0, 0)))


def kernel(x_prompt, x_sample, cache_k, cache_v, cache_idx_k, w_in, idx_k_g, idx_k_b, sgu_ln_g, sgu_ln_b,
           w_s, b_s, w_pa, w_pb, w_out, ln1_g, ln1_b, w_ff1, w_ff2, ln2_g, ln2_b):
    bp, tp, _ = x_prompt.shape
    bs, ts, _ = x_sample.shape
    depth = w_in.shape[0]
    past = cache_k.shape[2]
    topk_p = min(TOPK_MAX, tp // 4)
    topk_s = min(TOPK_MAX, (past + ts) // 4)
    tm_p = KEY_TILE
    tm_s = bs * ts
    assert (bp * tp) % tm_p == 0 and tp % KEY_TILE == 0
    s_all = past + ts
    s_pad = -(-s_all // KEY_TILE) * KEY_TILE
    qb_p = Q_BLOCK_PROMPT if tp % Q_BLOCK_PROMPT == 0 else Q_BLOCK

    tabs_p = _rope_tables(jnp.arange(tp), tm_p)
    tabs_s = _rope_tables(past + jnp.arange(ts), tm_s)

    xp = x_prompt.reshape(bp * tp, D_MODEL)
    xs = x_sample.reshape(bs * ts, D_MODEL)
    pk, pv, pik, sk, sv, sik, ssv = [], [], [], [], [], [], []
    row3 = lambda a: a.reshape(depth, 1, -1)
    w_a, w_ga, w_gb = _split_w_in(w_in)
    kig = row3(jnp.pad(idx_k_g, ((0, 0), (0, LANES - IDX_DIM))))
    kib = row3(jnp.pad(idx_k_b, ((0, 0), (0, LANES - IDX_DIM))))
    sg, sb = row3(sgu_ln_g), row3(sgu_ln_b)
    wpa, wpb, wout = w_pa.astype(BF16), w_pb.astype(BF16), w_out.astype(BF16)
    w1, w2 = w_ff1.astype(BF16), w_ff2.astype(BF16)
    g1, b1, g2, b2 = row3(ln1_g), row3(ln1_b), row3(ln2_g), row3(ln2_b)
    bst_p = b_s.transpose(0, 2, 1)
    ws_s, bst_s = w_s[:, :, :ts, :ts], b_s[:, :, :ts].transpose(0, 2, 1)
    for l in range(depth):
        qp, kf, vf, kb, vt, qia, kiw, kihl, ob = _inproj(
            xp, w_a, tabs_p, kig, kib, sg, sb, w_s, bst_p, layer=l, tm=tm_p, cl=GMLP_CHUNK, emit_vn=False,
            v_transposed=True)
        r3 = lambda a: a.reshape(bp, tp, a.shape[-1])
        oa = _attention(_head_weights_t(r3(kiw), qb_p), r3(qp), r3(qia), r3(kihl), r3(kb),
                        vt.reshape(bp, tp // KEY_TILE, N_KV_HEADS * VT_ROWS, KEY_TILE),
                        qb=qb_p, n_valid=qb_p, pos0=0, s_true=tp, topk=topk_p)
        xp = _post_block(xp, oa.reshape(bp * tp, ATTN_DIM), ob, w_ga, w_gb, wpa, wpb, wout, g1, b1,
                         w1, w2, g2, b2, layer=l, tm=tm_p)
        pk.append(kf.reshape(bp, tp, N_KV_HEADS, HEAD_DIM))
        pv.append(vf.reshape(bp, tp, N_KV_HEADS, HEAD_DIM))
        pik.append(kiw[:, :IDX_DIM].reshape(bp, tp, IDX_DIM))

        qp, kf, vf, kb, vb, qia, kiw, kihl, ob, vn = _inproj(
            xs, w_a, tabs_s, kig, kib, sg, sb, ws_s, bst_s, layer=l,
            tm=tm_s, cl=ts, emit_vn=True, v_transposed=False)
        r3 = lambda a: a.reshape(bs, ts, a.shape[-1])
        k_all = _pad_keys(jnp.concatenate([cache_k[l].reshape(bs, past, KV_DIM).astype(BF16), r3(kb)], 1), s_pad)
        v_all = _pad_keys(jnp.concatenate([cache_v[l].reshape(bs, past, KV_DIM).astype(BF16), r3(vb)], 1), s_pad)
        vt_all = _augment_vt(v_all.reshape(bs, s_pad // KEY_TILE, KEY_TILE, KV_DIM).transpose(0, 1, 3, 2))
        ki_all = _pad_keys(jnp.concatenate([_split_hi_lo(cache_idx_k[l]), r3(kihl)], 1), s_pad)
        padq = lambda a: jnp.pad(r3(a), ((0, 0), (0, Q_BLOCK - ts), (0, 0)))
        oa = _attention(_head_weights_t(padq(kiw), Q_BLOCK), padq(qp), padq(qia), ki_all, k_all, vt_all,
                        qb=Q_BLOCK, n_valid=ts, pos0=past, s_true=s_all, topk=topk_s)[:, :ts]
        xs = _post_block(xs, oa.reshape(bs * ts, ATTN_DIM), ob, w_ga, w_gb, wpa, wpb, wout, g1, b1,
                         w1, w2, g2, b2, layer=l, tm=tm_s)
        sk.append(kf.reshape(bs, ts, N_KV_HEADS, HEAD_DIM))
        sv.append(vf.reshape(bs, ts, N_KV_HEADS, HEAD_DIM))
        sik.append(kiw[:, :IDX_DIM].reshape(bs, ts, IDX_DIM))
        ssv.append(vn.reshape(bs, ts, GMLP_DIM))

    return (xp.reshape(bp, tp, D_MODEL), xs.reshape(bs, ts, D_MODEL),
            jnp.stack(pk), jnp.stack(pv), jnp.stack(pik),
            jnp.stack(sk), jnp.stack(sv), jnp.stack(sik), jnp.stack(ssv))
```

```python
import functools

import jax
import jax.numpy as jnp
from jax import lax
from jax.experimental import pallas as pl
from jax.experimental.pallas import tpu as pltpu

F32 = jnp.float32
BF16 = jnp.bfloat16

D_MODEL = 1024
N_HEADS = 8
N_KV_HEADS = 2
HEAD_DIM = 64
ATTN_DIM = N_HEADS * HEAD_DIM
KV_DIM = N_KV_HEADS * HEAD_DIM
IDX_HEADS = 4
IDX_DIM = 64
TOPK_MAX = 256
CHUNK = 64
GMLP_CHUNK = 128
GMLP_GROUPS = 4
GMLP_DIM = 512
GMLP_GROUP_DIM = GMLP_DIM // GMLP_GROUPS
D_FF = 4 * D_MODEL
ROPE_THETA = 500000.0
ROPE_DIM = HEAD_DIM // 4
DEPTH = 2
ALPHA = (2 * DEPTH) ** 0.25
LN_EPS = 1e-5

LANES = 128
SUBLANES = 8
Q_SCALE = HEAD_DIM ** -0.5
IDX_SCALE = IDX_DIM ** -0.5
WI_SCALE = IDX_HEADS ** -0.5
NEG_INF = float("-inf")
M_INIT = -1e30

OFF_Q, OFF_K, OFF_V, OFF_QI, OFF_KIW, OFF_U, OFF_VG = 0, 512, 640, 768, 1024, 1152, 1664
KEY_TILE = 512
Q_BLOCK = LANES
Q_BLOCK_PROMPT = 2 * LANES
SEARCH_CHUNK = 256
SEARCH_UNROLL = 4
VT_ROWS = 80
VMEM_LIMIT = 56 * 1024 * 1024


def _cparams(sem):
    return pltpu.CompilerParams(dimension_semantics=sem, vmem_limit_bytes=VMEM_LIMIT)


def _layer_norm(x, g, b):
    mu = jnp.mean(x, axis=-1, keepdims=True)
    d = x - mu
    var = jnp.mean(d * d, axis=-1, keepdims=True)
    return d * lax.rsqrt(var + LN_EPS) * g + b


def _rope(y, cos, sn, sp):
    w = y.shape[-1]
    half = ROPE_DIM // 2
    return y * cos + pltpu.roll(y, w - half, 1) * sn + pltpu.roll(y, half, 1) * sp


def _tile_lanes(a, n):
    return a if n == 1 else jnp.concatenate([a] * n, axis=1)


def _inproj_body(x_ref, w_ref, cos_ref, sn_ref, sp_ref, kig_ref, kib_ref, sg_ref, sb_ref,
                 ws_ref, bst_ref,
                 qp_ref, kf_ref, vf_ref, kb_ref, vb_ref, qia_ref, kiw_ref, kihl_ref, ob_ref,
                 *vn_refs, cl, v_transposed):
    tm = x_ref.shape[0]
    xb = x_ref[...].astype(BF16)

    def proj(lo, n):
        return jnp.dot(xb, w_ref[:, lo:lo + n], preferred_element_type=F32)

    cos, sn, sp = cos_ref[...], sn_ref[...], sp_ref[...]
    lane = lax.broadcasted_iota(jnp.int32, (tm, LANES), 1)
    left = lane < IDX_DIM

    q = _rope(proj(OFF_Q, ATTN_DIM), _tile_lanes(cos, 4), _tile_lanes(sn, 4), _tile_lanes(sp, 4)) * Q_SCALE
    heads_per_kv = N_HEADS // N_KV_HEADS
    for j in range(ATTN_DIM // LANES):
        tile = q[:, j * LANES:(j + 1) * LANES]
        rolled = pltpu.roll(tile, HEAD_DIM, 1)
        on_left = (2 * j) // heads_per_kv == 0
        keep = left if on_left else jnp.logical_not(left)
        even = jnp.where(keep, tile if on_left else rolled, 0.0)
        odd = jnp.where(keep, rolled if on_left else tile, 0.0)
        qp_ref[:, (2 * j) * LANES:(2 * j + 1) * LANES] = even.astype(BF16)
        qp_ref[:, (2 * j + 1) * LANES:(2 * j + 2) * LANES] = odd.astype(BF16)

    k = _rope(proj(OFF_K, KV_DIM), cos, sn, sp)
    kf_ref[...] = k
    kb_ref[...] = k.astype(BF16)
    v = proj(OFF_V, KV_DIM)
    vf_ref[...] = v
    vb = v.astype(BF16)
    if v_transposed:
        r_e = lax.broadcasted_iota(jnp.int32, (N_KV_HEADS * VT_ROWS, LANES), 0)
        c_e = lax.broadcasted_iota(jnp.int32, (N_KV_HEADS * VT_ROWS, LANES), 1)
        head, d = r_e // VT_ROWS, r_e % VT_ROWS
        sel = jnp.where((d < HEAD_DIM) & (c_e == head * HEAD_DIM + d), 1.0, 0.0).astype(BF16)
        vt = lax.dot_general(sel, vb, (((1,), (1,)), ((), ())), preferred_element_type=F32)
        r_o = lax.broadcasted_iota(jnp.int32, (N_KV_HEADS * VT_ROWS, tm), 0)
        vb_ref[...] = jnp.where(r_o % VT_ROWS == HEAD_DIM, 1.0, vt).astype(BF16)
    else:
        vb_ref[...] = vb

    qi = _rope(proj(OFF_QI, IDX_HEADS * IDX_DIM), _tile_lanes(cos, 2), _tile_lanes(sn, 2),
               _tile_lanes(sp, 2)) * IDX_SCALE
    qi_hi = qi.astype(BF16).astype(F32)
    qi_lo = qi - qi_hi
    for t in range(IDX_HEADS // 2):
        h_t = qi_hi[:, t * LANES:(t + 1) * LANES]
        l_t = qi_lo[:, t * LANES:(t + 1) * LANES]
        a0 = jnp.where(left, h_t, pltpu.roll(l_t, IDX_DIM, 1))
        a1 = jnp.where(left, pltpu.roll(h_t, IDX_DIM, 1), l_t)
        for j, a in enumerate((a0, a1)):
            base = (2 * t + j) * 2 * LANES
            qia_ref[:, base:base + 2 * LANES] = jnp.concatenate([a, a], axis=1).astype(BF16)

    slab = proj(OFF_KIW, LANES)
    mu = jnp.sum(jnp.where(left, slab, 0.0), axis=1, keepdims=True) * (1.0 / IDX_DIM)
    d = jnp.where(left, slab - mu, 0.0)
    var = jnp.sum(d * d, axis=1, keepdims=True) * (1.0 / IDX_DIM)
    kin = d * lax.rsqrt(var + LN_EPS) * kig_ref[...] + kib_ref[...]
    slab = jnp.where(left, kin, slab)
    slab = _rope(slab, jnp.where(left, cos, 1.0), jnp.where(left, sn, 0.0), jnp.where(left, sp, 0.0))
    kiw_ref[...] = slab
    kz = jnp.where(left, slab, 0.0)
    k_hi = kz.astype(BF16).astype(F32)
    k_lo = kz - k_hi
    hh = k_hi + pltpu.roll(k_hi, IDX_DIM, 1)
    ll = k_lo + pltpu.roll(k_lo, IDX_DIM, 1)
    kihl_ref[...] = jnp.concatenate([hh, ll], axis=1).astype(BF16)

    u = proj(OFF_U, GMLP_DIM)
    vn = _layer_norm(proj(OFF_VG, GMLP_DIM), sg_ref[...], sb_ref[...])
    if vn_refs:
        vn_refs[0][...] = vn
    vnb = vn.astype(BF16)
    r_i = lax.broadcasted_iota(jnp.int32, (cl, cl), 0)
    c_i = lax.broadcasted_iota(jnp.int32, (cl, cl), 1)
    for g in range(GMLP_GROUPS):
        wg = jnp.where(r_i >= c_i, ws_ref[g], 0.0).astype(BF16)
        bg = bst_ref[:, g:g + 1]
        gs = slice(g * GMLP_GROUP_DIM, (g + 1) * GMLP_GROUP_DIM)
        for c in range(tm // cl):
            rs = slice(c * cl, (c + 1) * cl)
            mix = jnp.dot(wg, vnb[rs, gs], preferred_element_type=F32) + bg
            ob_ref[rs, gs] = (u[rs, gs] * mix).astype(BF16)


def _layer_spec(a, layer):
    return pl.BlockSpec((None,) + a.shape[1:], lambda i: (layer,) + (0,) * (a.ndim - 1),
                        pipeline_mode=pl.Buffered(1))


def _inproj(x2d, w_a, tabs, kig, kib, sg, sb, ws, bst, *, layer, tm, cl, emit_vn, v_transposed):
    m = x2d.shape[0]
    cos, sn, sp = tabs
    nt = cos.shape[0] // tm
    row = lambda w: pl.BlockSpec((tm, w), lambda i: (i, 0))
    tab = pl.BlockSpec((tm, LANES), lambda i: (i % nt, 0))
    full = lambda a: _layer_spec(a, layer)
    if v_transposed:
        vb_shape = jax.ShapeDtypeStruct((m // tm, N_KV_HEADS * VT_ROWS, tm), BF16)
        vb_spec = pl.BlockSpec((None, N_KV_HEADS * VT_ROWS, tm), lambda i: (i, 0, 0))
    else:
        vb_shape, vb_spec = jax.ShapeDtypeStruct((m, KV_DIM), BF16), row(KV_DIM)
    out_shapes = [
        jax.ShapeDtypeStruct((m, N_HEADS * LANES), BF16),
        jax.ShapeDtypeStruct((m, KV_DIM), F32),
        jax.ShapeDtypeStruct((m, KV_DIM), F32),
        jax.ShapeDtypeStruct((m, KV_DIM), BF16),
        vb_shape,
        jax.ShapeDtypeStruct((m, IDX_HEADS * 4 * IDX_DIM), BF16),
        jax.ShapeDtypeStruct((m, LANES), F32),
        jax.ShapeDtypeStruct((m, 4 * IDX_DIM), BF16),
        jax.ShapeDtypeStruct((m, GMLP_DIM), BF16),
    ]
    out_specs = [row(N_HEADS * LANES), row(KV_DIM), row(KV_DIM), row(KV_DIM), vb_spec,
                 row(IDX_HEADS * 4 * IDX_DIM), row(LANES), row(4 * IDX_DIM), row(GMLP_DIM)]
    if emit_vn:
        out_shapes.append(jax.ShapeDtypeStruct((m, GMLP_DIM), F32))
        out_specs.append(row(GMLP_DIM))
    return pl.pallas_call(
        functools.partial(_inproj_body, cl=cl, v_transposed=v_transposed),
        grid=(m // tm,),
        in_specs=[row(D_MODEL), full(w_a), tab, tab, tab, full(kig), full(kib), full(sg), full(sb),
                  full(ws), full(bst)],
        out_specs=out_specs,
        out_shape=out_shapes,
        compiler_params=_cparams(("parallel",)),
        name="inproj",
    )(x2d, w_a, cos, sn, sp, kig, kib, sg, sb, ws, bst)


def _nt_dot(a, b):
    return lax.dot_general(a, b, (((1,), (1,)), ((), ())), preferred_element_type=F32)


def _tree_sum(parts):
    while len(parts) > 1:
        parts = [parts[j] + parts[j + 1] for j in range(0, len(parts) - 1, 2)] + (
            [parts[-1]] if len(parts) % 2 else [])
    return parts[0]


def _attn_body(wit_ref, qp_ref, qia_ref, kihl_ref, kb_ref, vt_ref, o_ref,
               keys_ref, bias_ref, s_ref, *, qb, n_valid, pos0, s_true, topk):
    i = pl.program_id(1)
    sc_n = SEARCH_CHUNK
    cpt = KEY_TILE // sc_n
    grp = N_HEADS // N_KV_HEADS
    qlane = lax.broadcasted_iota(jnp.int32, (1, qb), 1)
    qpos = pos0 + i * qb + qlane
    limit = jnp.minimum((qpos // CHUNK + 1) * CHUNK, s_true)
    kv_len = jnp.minimum(((pos0 + i * qb + qb - 1) // CHUNK + 1) * CHUNK, s_true)
    n_kt = (kv_len + KEY_TILE - 1) // KEY_TILE
    n_c = (kv_len + sc_n - 1) // sc_n
    krow = lax.broadcasted_iota(jnp.int32, (sc_n, 1), 0)

    wit = wit_ref[...] * WI_SCALE
    hw = 4 * IDX_DIM
    pairs = [jnp.concatenate([qia_ref[:, (2 * hp) * hw:(2 * hp + 1) * hw],
                              qia_ref[:, (2 * hp + 1) * hw:(2 * hp + 2) * hw]], axis=0)
             for hp in range(IDX_HEADS // 2)]

    def score_tiles(tiles):
        for t in tiles:
            kt = kihl_ref[pl.ds(pl.multiple_of(t * KEY_TILE, KEY_TILE), KEY_TILE), :]
            sc = jnp.zeros((KEY_TILE, qb), F32)
            for hp, a2 in enumerate(pairs):
                dots = jnp.maximum(_nt_dot(kt, a2), 0.0)
                sc = (sc + wit[2 * hp:2 * hp + 1, :] * dots[:, :qb]
                      + wit[2 * hp + 1:2 * hp + 2, :] * dots[:, qb:])
            for j in range(cpt):
                kidx = t * KEY_TILE + j * sc_n + krow
                s_j = jnp.where(kidx < limit, sc[j * sc_n:(j + 1) * sc_n], NEG_INF)
                s_j = jnp.where(s_j == 0.0, 0.0, s_j)
                bits = lax.bitcast_convert_type(s_j, jnp.int32)
                keys_ref[t * cpt + j] = bits ^ ((bits >> 31) & 0x7FFFFFFF)

    n_pair = n_kt // 2
    odd = n_kt % 2 == 1

    def score_pair(pr, carry):
        score_tiles((2 * pr, 2 * pr + 1))
        return carry

    lax.fori_loop(0, n_pair, score_pair, 0)
    pl.when(odd)(lambda: score_tiles((n_kt - 1,)))

    n_acc = 4

    def count(pred, thr):
        def one_chunk(c, accs):
            ones = jnp.where(pred(keys_ref[c], thr), 1.0, 0.0)
            rows = [ones[SUBLANES * j:SUBLANES * (j + 1)] for j in range(sc_n // SUBLANES)]
            per = len(rows) // n_acc
            return tuple(a + _tree_sum(rows[k * per:(k + 1) * per]) for k, a in enumerate(accs))

        def many_chunks(g, accs):
            for j in range(SEARCH_UNROLL):
                accs = one_chunk(g * SEARCH_UNROLL + j, accs)
            return accs

        n_big = n_c // SEARCH_UNROLL
        accs = tuple(jnp.zeros((SUBLANES, qb), F32) for _ in range(n_acc))
        accs = lax.fori_loop(0, n_big, many_chunks, accs)
        accs = lax.fori_loop(n_big * SEARCH_UNROLL, n_c, one_chunk, accs)
        return jnp.sum(_tree_sum(list(accs)), axis=0, keepdims=True)

    def bit_step(it, carry):
        thr, best = carry
        cand = thr + jnp.left_shift(jnp.int32(1), 31 - it)
        cnt = count(lambda kk, t_: kk >= t_, cand)
        ok = cnt >= topk
        return jnp.where(ok, cand, thr), jnp.where(ok, cnt, best)

    thr, n_ge = lax.fori_loop(0, 32, bit_step,
                              (jnp.full((1, qb), -2 ** 31, jnp.int32), jnp.zeros((1, qb), F32)))

    real = qlane < n_valid
    overflow = jnp.max(jnp.where(real, n_ge, 0.0)) > topk

    @pl.when(jnp.logical_not(overflow))
    def _():
        def body(c, carry):
            sel = (keys_ref[c] >= thr) & (c * sc_n + krow < limit)
            bias_ref[c] = jnp.where(sel, 0.0, NEG_INF)
            return carry
        lax.fori_loop(0, n_kt * cpt, body, 0)

    @pl.when(overflow)
    def _():
        need = topk - count(lambda kk, t_: kk > t_, thr)
        r_i = lax.broadcasted_iota(jnp.int32, (sc_n, sc_n), 0)
        c_i = lax.broadcasted_iota(jnp.int32, (sc_n, sc_n), 1)
        lower = jnp.where(c_i <= r_i, 1.0, 0.0).astype(BF16)

        def tie_chunk(c, off):
            kk = keys_ref[c]
            eq = kk == thr
            local = jnp.dot(lower, jnp.where(eq, 1.0, 0.0).astype(BF16), preferred_element_type=F32)
            take = eq & (local + off <= need)
            sel = ((kk > thr) | take) & (c * sc_n + krow < limit)
            bias_ref[c] = jnp.where(sel, 0.0, NEG_INF)
            return off + local[sc_n - 1:sc_n, :]

        lax.fori_loop(0, n_kt * cpt, tie_chunk, jnp.zeros((1, qb), F32))

    r_e = lax.broadcasted_iota(jnp.int32, (LANES, LANES), 0)
    c_e = lax.broadcasted_iota(jnp.int32, (LANES, LANES), 1)
    eye = jnp.where(r_e == c_e, 1.0, 0.0).astype(BF16)
    kv_heads = range(N_KV_HEADS)
    for part in range(qb // LANES):
        q0 = part * LANES
        part_len = jnp.minimum(((pos0 + i * qb + q0 + LANES - 1) // CHUNK + 1) * CHUNK, s_true)
        part_kt = (part_len + KEY_TILE - 1) // KEY_TILE
        part_pair = part_kt // 2
        part_odd = part_kt % 2 == 1
        qns = [jnp.concatenate(
            [qp_ref[q0:q0 + LANES, (grp * n + g) * LANES:(grp * n + g + 1) * LANES] for g in range(grp)],
            axis=0) for n in kv_heads]

        def pass_a(tiles, ms, q0=q0, qns=qns):
            ms = list(ms)
            for t in tiles:
                kt = kb_ref[pl.ds(pl.multiple_of(t * KEY_TILE, KEY_TILE), KEY_TILE), :]
                bias = jnp.concatenate([bias_ref[t * cpt + j, :, q0:q0 + LANES] for j in range(cpt)], axis=0)
                bias = jnp.concatenate([bias] * grp, axis=1)
                for n in kv_heads:
                    s = _nt_dot(kt, qns[n]) + bias
                    s_ref[n, t] = s
                    ms[n] = jnp.maximum(ms[n], jnp.max(s, axis=0, keepdims=True))
            return tuple(ms)

        def pass_b(tiles, accs, ms):
            accs = list(accs)
            for t in tiles:
                for n in kv_heads:
                    p = jnp.exp(s_ref[n, t] - ms[n]).astype(BF16)
                    vt = vt_ref[t, n * VT_ROWS:(n + 1) * VT_ROWS, :]
                    accs[n] = accs[n] + jnp.dot(vt, p, preferred_element_type=F32)
            return tuple(accs)

        ms = tuple(jnp.full((1, grp * LANES), M_INIT, F32) for _ in kv_heads)
        ms = lax.fori_loop(0, part_pair, lambda pr, c: pass_a((2 * pr, 2 * pr + 1), c), ms)
        ms = lax.cond(part_odd, lambda c: pass_a((part_kt - 1,), c), lambda c: c, ms)
        accs = tuple(jnp.zeros((VT_ROWS, grp * LANES), F32) for _ in kv_heads)
        accs = lax.fori_loop(0, part_pair, lambda pr, c, ms=ms: pass_b((2 * pr, 2 * pr + 1), c, ms), accs)
        accs = lax.cond(part_odd, lambda c, ms=ms: pass_b((part_kt - 1,), c, ms), lambda c: c, accs)

        for n in kv_heads:
            acc = accs[n]
            ot = (acc[:HEAD_DIM] / acc[HEAD_DIM:HEAD_DIM + 1]).astype(BF16)
            for g in range(grp):
                h = grp * n + g
                o_ref[q0:q0 + LANES, h * HEAD_DIM:(h + 1) * HEAD_DIM] = _nt_dot(
                    eye, ot[:, g * LANES:(g + 1) * LANES]).astype(o_ref.dtype)


def _attention(wit, qp, qia, kihl, kb, vt, *, qb, n_valid, pos0, s_true, topk):
    b, t, _ = qp.shape
    s_pad = kihl.shape[1]
    assert t % qb == 0 and s_pad % KEY_TILE == 0
    grp = N_HEADS // N_KV_HEADS
    qblk = lambda w: pl.BlockSpec((None, qb, w), lambda bi, i: (bi, i, 0))
    kblk = lambda w: pl.BlockSpec((None, s_pad, w), lambda bi, i: (bi, 0, 0))
    return pl.pallas_call(
        functools.partial(_attn_body, qb=qb, n_valid=n_valid, pos0=pos0, s_true=s_true, topk=topk),
        grid=(b, t // qb),
        in_specs=[pl.BlockSpec((None, None, SUBLANES, qb), lambda bi, i: (bi, i, 0, 0)),
                  qblk(N_HEADS * LANES), qblk(IDX_HEADS * 4 * IDX_DIM),
                  kblk(4 * IDX_DIM), kblk(KV_DIM),
                  pl.BlockSpec((None, s_pad // KEY_TILE, N_KV_HEADS * VT_ROWS, KEY_TILE),
                               lambda bi, i: (bi, 0, 0, 0))],
        out_specs=qblk(ATTN_DIM),
        out_shape=jax.ShapeDtypeStruct((b, t, ATTN_DIM), BF16),
        scratch_shapes=[
            pltpu.VMEM((s_pad // SEARCH_CHUNK, SEARCH_CHUNK, qb), jnp.int32),
            pltpu.VMEM((s_pad // SEARCH_CHUNK, SEARCH_CHUNK, qb), F32),
            pltpu.VMEM((N_KV_HEADS, s_pad // KEY_TILE, KEY_TILE, grp * LANES), F32),
        ],
        compiler_params=_cparams(("parallel", "arbitrary")),
        name="sparse_attn",
    )(wit, qp, qia, kihl, kb, vt)


FF_SPLIT = 4


def _post_body(x_ref, oa_ref, ob_ref, wga_ref, wgb_ref, wpa_ref, wpb_ref, wout_ref, g1_ref, b1_ref,
               w1_ref, w2_ref, g2_ref, b2_ref, o_ref):
    x = x_ref[...]
    xb = x.astype(BF16)
    ga = jnp.dot(xb, wga_ref[...], preferred_element_type=F32)
    gb = jnp.dot(xb, wgb_ref[...], preferred_element_type=F32)
    pa = jnp.dot(oa_ref[...], wpa_ref[...], preferred_element_type=F32)
    pb = jnp.dot(ob_ref[...], wpb_ref[...], preferred_element_type=F32)
    merged = jax.nn.sigmoid(ga) * pa + jax.nn.sigmoid(gb) * pb
    h = ALPHA * x + jnp.dot(merged.astype(BF16), wout_ref[...], preferred_element_type=F32)
    x1 = _layer_norm(h, g1_ref[...], b1_ref[...])
    x1b = x1.astype(BF16)
    cw = D_FF // FF_SPLIT
    ff = jnp.zeros(x.shape, F32)
    for c in range(FF_SPLIT):
        hcol = jnp.maximum(jnp.dot(x1b, w1_ref[:, c * cw:(c + 1) * cw], preferred_element_type=F32), 0.0)
        ff = ff + jnp.dot((hcol * hcol).astype(BF16), w2_ref[c * cw:(c + 1) * cw, :],
                          preferred_element_type=F32)
    o_ref[...] = _layer_norm(ALPHA * x1 + ff, g2_ref[...], b2_ref[...])


def _post_block(x2d, oa, ob, wga, wgb, wpa, wpb, wout, g1, b1, w1, w2, g2, b2, *, layer, tm):
    m = x2d.shape[0]
    row = lambda w: pl.BlockSpec((tm, w), lambda i: (i, 0))
    full = lambda a: _layer_spec(a, layer)
    consts = (wga, wgb, wpa, wpb, wout, g1, b1, w1, w2, g2, b2)
    return pl.pallas_call(
        _post_body,
        grid=(m // tm,),
        in_specs=[row(D_MODEL), row(ATTN_DIM), row(GMLP_DIM)] + [full(a) for a in consts],
        out_specs=row(D_MODEL),
        out_shape=jax.ShapeDtypeStruct((m, D_MODEL), F32),
        compiler_params=_cparams(("parallel",)),
        name="post_block",
    )(x2d, oa, ob, *consts)


def _rope_tables(pos, rows):
    half = ROPE_DIM // 2
    freqs = ROPE_THETA ** (-jnp.arange(half, dtype=F32) * 2.0 / ROPE_DIM)
    ang = pos.astype(F32)[:, None] * freqs[None, :]
    cos, sin = jnp.cos(ang), jnp.sin(ang)
    t = pos.shape[0]
    rest = HEAD_DIM - ROPE_DIM
    cos_h = jnp.concatenate([cos, cos, jnp.ones((t, rest), F32)], axis=1)
    sn_h = jnp.concatenate([-sin, jnp.zeros((t, half + rest), F32)], axis=1)
    sp_h = jnp.concatenate([jnp.zeros((t, half), F32), sin, jnp.zeros((t, rest), F32)], axis=1)
    reps = (max(rows // t, 1), LANES // HEAD_DIM)
    return tuple(jnp.tile(a, reps) for a in (cos_h, sn_h, sp_h))


def _split_w_in(w):
    wb = w.astype(BF16)
    n_head = OFF_KIW + IDX_DIM + IDX_HEADS
    col = lax.broadcasted_iota(jnp.int32, (1, 1, OFF_U), 2)
    head = jnp.where(col < n_head, wb[..., :OFF_U], jnp.zeros((), BF16))
    tail = wb[..., n_head:]
    w_a = jnp.concatenate([head, tail[..., :2 * GMLP_DIM]], axis=-1)
    return w_a, tail[..., 2 * GMLP_DIM:2 * GMLP_DIM + D_MODEL], tail[..., 2 * GMLP_DIM + D_MODEL:]


def _split_hi_lo(ki):
    hi = ki.astype(BF16)
    lo = (ki - hi.astype(F32)).astype(BF16)
    return jnp.concatenate([hi, hi, lo, lo], axis=-1)


def _head_weights_t(kiw, qb):
    b, t, _ = kiw.shape
    wi = kiw[:, :, IDX_DIM:IDX_DIM + IDX_HEADS].reshape(b, t // qb, qb, IDX_HEADS)
    return jnp.pad(wi.transpose(0, 1, 3, 2), ((0, 0), (0, 0), (0, SUBLANES - IDX_HEADS), (0, 0)))


def _augment_vt(vt):
    b, nt, _, kt = vt.shape
    v4 = vt.reshape(b, nt, N_KV_HEADS, HEAD_DIM, kt)
    ones = jnp.ones((b, nt, N_KV_HEADS, 1, kt), vt.dtype)
    zeros = jnp.zeros((b, nt, N_KV_HEADS, VT_ROWS - HEAD_DIM - 1, kt), vt.dtype)
    return jnp.concatenate([v4, ones, zeros], axis=3).reshape(b, nt, N_KV_HEADS * VT_ROWS, kt)


def _pad_keys(a, s_pad):
    return jnp.pad(a, ((0, 0), (0, s_pad - a.shape[1]), (0, 0)))


def kernel(x_prompt, x_sample, cache_k, cache_v, cache_idx_k, w_in, idx_k_g, idx_k_b, sgu_ln_g, sgu_ln_b,
           w_s, b_s, w_pa, w_pb, w_out, ln1_g, ln1_b, w_ff1, w_ff2, ln2_g, ln2_b):
    bp, tp, _ = x_prompt.shape
    bs, ts, _ = x_sample.shape
    depth = w_in.shape[0]
    past = cache_k.shape[2]
    topk_p = min(TOPK_MAX, tp // 4)
    topk_s = min(TOPK_MAX, (past + ts) // 4)
    tm_p = KEY_TILE
    tm_s = bs * ts
    assert (bp * tp) % tm_p == 0 and tp % KEY_TILE == 0
    s_all = past + ts
    s_pad = -(-s_all // KEY_TILE) * KEY_TILE
    qb_p = Q_BLOCK_PROMPT if tp % Q_BLOCK_PROMPT == 0 else Q_BLOCK

    tabs_p = _rope_tables(jnp.arange(tp), tm_p)
    tabs_s = _rope_tables(past + jnp.arange(ts), tm_s)

    xp = x_prompt.reshape(bp * tp, D_MODEL)
    xs = x_sample.reshape(bs * ts, D_MODEL)
    pk, pv, pik, sk, sv, sik, ssv = [], [], [], [], [], [], []
    row3 = lambda a: a.reshape(depth, 1, -1)
    w_a, w_ga, w_gb = _split_w_in(w_in)
    kig = row3(jnp.pad(idx_k_g, ((0, 0), (0, LANES - IDX_DIM))))
    kib = row3(jnp.pad(idx_k_b, ((0, 0), (0, LANES - IDX_DIM))))
    sg, sb = row3(sgu_ln_g), row3(sgu_ln_b)
    wpa, wpb, wout = w_pa.astype(BF16), w_pb.astype(BF16), w_out.astype(BF16)
    w1, w2 = w_ff1.astype(BF16), w_ff2.astype(BF16)
    g1, b1, g2, b2 = row3(ln1_g), row3(ln1_b), row3(ln2_g), row3(ln2_b)
    bst_p = b_s.transpose(0, 2, 1)
    ws_s, bst_s = w_s[:, :, :ts, :ts], b_s[:, :, :ts].transpose(0, 2, 1)
    for l in range(depth):
        qp, kf, vf, kb, vt, qia, kiw, kihl, ob = _inproj(
            xp, w_a, tabs_p, kig, kib, sg, sb, w_s, bst_p, layer=l, tm=tm_p, cl=GMLP_CHUNK, emit_vn=False,
            v_transposed=True)
        r3 = lambda a: a.reshape(bp, tp, a.shape[-1])
        oa = _attention(_head_weights_t(r3(kiw), qb_p), r3(qp), r3(qia), r3(kihl), r3(kb),
                        vt.reshape(bp, tp // KEY_TILE, N_KV_HEADS * VT_ROWS, KEY_TILE),
                        qb=qb_p, n_valid=qb_p, pos0=0, s_true=tp, topk=topk_p)
        xp = _post_block(xp, oa.reshape(bp * tp, ATTN_DIM), ob, w_ga, w_gb, wpa, wpb, wout, g1, b1,
                         w1, w2, g2, b2, layer=l, tm=tm_p)
        pk.append(kf.reshape(bp, tp, N_KV_HEADS, HEAD_DIM))
        pv.append(vf.reshape(bp, tp, N_KV_HEADS, HEAD_DIM))
        pik.append(kiw[:, :IDX_DIM].reshape(bp, tp, IDX_DIM))

        qp, kf, vf, kb, vb, qia, kiw, kihl, ob, vn = _inproj(
            xs, w_a, tabs_s, kig, kib, sg, sb, ws_s, bst_s, layer=l,
            tm=tm_s, cl=ts, emit_vn=True, v_transposed=False)
        r3 = lambda a: a.reshape(bs, ts, a.shape[-1])
        k_all = _pad_keys(jnp.concatenate([cache_k[l].reshape(bs, past, KV_DIM).astype(BF16), r3(kb)], 1), s_pad)
        v_all = _pad_keys(jnp.concatenate([cache_v[l].reshape(bs, past, KV_DIM).astype(BF16), r3(vb)], 1), s_pad)
        vt_all = _augment_vt(v_all.reshape(bs, s_pad // KEY_TILE, KEY_TILE, KV_DIM).transpose(0, 1, 3, 2))
        ki_all = _pad_keys(jnp.concatenate([_split_hi_lo(cache_idx_k[l]), r3(kihl)], 1), s_pad)
        padq = lambda a: jnp.pad(r3(a), ((0, 0), (0, Q_BLOCK - ts), (0, 0)))
        oa = _attention(_head_weights_t(padq(kiw), Q_BLOCK), padq(qp), padq(qia), ki_all, k_all, vt_all,
                        qb=Q_BLOCK, n_valid=ts, pos0=past, s_true=s_all, topk=topk_s)[:, :ts]
        xs = _post_block(xs, oa.reshape(bs * ts, ATTN_DIM), ob, w_ga, w_gb, wpa, wpb, wout, g1, b1,
                         w1, w2, g2, b2, layer=l, tm=tm_s)
        sk.append(kf.reshape(bs, ts, N_KV_HEADS, HEAD_DIM))
        sv.append(vf.reshape(bs, ts, N_KV_HEADS, HEAD_DIM))
        sik.append(kiw[:, :IDX_DIM].reshape(bs, ts, IDX_DIM))
        ssv.append(vn.reshape(bs, ts, GMLP_DIM))

    return (xp.reshape(bp, tp, D_MODEL), xs.reshape(bs, ts, D_MODEL),
            jnp.stack(pk), jnp.stack(pv), jnp.stack(pik),
            jnp.stack(sk), jnp.stack(sv), jnp.stack(sik), jnp.stack(ssv))
```

```python
import functools

import jax
import jax.numpy as jnp
from jax import lax
from jax.experimental import pallas as pl
from jax.experimental.pallas import tpu as pltpu

F32 = jnp.float32
BF16 = jnp.bfloat16

D_MODEL = 1024
N_HEADS = 8
N_KV_HEADS = 2
HEAD_DIM = 64
ATTN_DIM = N_HEADS * HEAD_DIM
KV_DIM = N_KV_HEADS * HEAD_DIM
IDX_HEADS = 4
IDX_DIM = 64
TOPK_MAX = 256
CHUNK = 64
GMLP_CHUNK = 128
GMLP_GROUPS = 4
GMLP_DIM = 512
GMLP_GROUP_DIM = GMLP_DIM // GMLP_GROUPS
D_FF = 4 * D_MODEL
ROPE_THETA = 500000.0
ROPE_DIM = HEAD_DIM // 4
DEPTH = 2
ALPHA = (2 * DEPTH) ** 0.25
LN_EPS = 1e-5

LANES = 128
SUBLANES = 8
Q_SCALE = HEAD_DIM ** -0.5 * 1.4426950408889634
IDX_SCALE = IDX_DIM ** -0.5
WI_SCALE = IDX_HEADS ** -0.5
NEG_INF = float("-inf")
M_INIT = -1e30

OFF_Q, OFF_K, OFF_V, OFF_QI, OFF_KIW, OFF_U, OFF_VG = 0, 512, 640, 768, 1024, 1152, 1664
KEY_TILE = 512
Q_BLOCK = LANES
Q_BLOCK_PROMPT = 2 * LANES
SEARCH_CHUNK = 256
SEARCH_UNROLL = 4
VT_ROWS = 80
VMEM_LIMIT = 56 * 1024 * 1024


def _cparams(sem):
    return pltpu.CompilerParams(dimension_semantics=sem, vmem_limit_bytes=VMEM_LIMIT)


def _layer_norm(x, g, b):
    mu = jnp.mean(x, axis=-1, keepdims=True)
    d = x - mu
    var = jnp.mean(d * d, axis=-1, keepdims=True)
    return d * lax.rsqrt(var + LN_EPS) * g + b


def _rope(y, cos, sn, sp):
    w = y.shape[-1]
    half = ROPE_DIM // 2
    return y * cos + pltpu.roll(y, w - half, 1) * sn + pltpu.roll(y, half, 1) * sp


def _tile_lanes(a, n):
    return a if n == 1 else jnp.concatenate([a] * n, axis=1)


def _inproj_body(x_ref, w_ref, cos_ref, sn_ref, sp_ref, kig_ref, kib_ref, sg_ref, sb_ref,
                 ws_ref, bst_ref,
                 qp_ref, kf_ref, vf_ref, kb_ref, vb_ref, qia_ref, kiw_ref, kihl_ref, ob_ref,
                 *vn_refs, cl, v_transposed):
    tm = x_ref.shape[0]
    xb = x_ref[...].astype(BF16)

    def proj(lo, n):
        return jnp.dot(xb, w_ref[:, lo:lo + n], preferred_element_type=F32)

    cos, sn, sp = cos_ref[...], sn_ref[...], sp_ref[...]
    lane = lax.broadcasted_iota(jnp.int32, (tm, LANES), 1)
    left = lane < IDX_DIM

    q = _rope(proj(OFF_Q, ATTN_DIM), _tile_lanes(cos, 4), _tile_lanes(sn, 4), _tile_lanes(sp, 4)) * Q_SCALE
    heads_per_kv = N_HEADS // N_KV_HEADS
    for j in range(ATTN_DIM // LANES):
        tile = q[:, j * LANES:(j + 1) * LANES]
        rolled = pltpu.roll(tile, HEAD_DIM, 1)
        on_left = (2 * j) // heads_per_kv == 0
        keep = left if on_left else jnp.logical_not(left)
        even = jnp.where(keep, tile if on_left else rolled, 0.0)
        odd = jnp.where(keep, rolled if on_left else tile, 0.0)
        qp_ref[:, (2 * j) * LANES:(2 * j + 1) * LANES] = even.astype(BF16)
        qp_ref[:, (2 * j + 1) * LANES:(2 * j + 2) * LANES] = odd.astype(BF16)

    k = _rope(proj(OFF_K, KV_DIM), cos, sn, sp)
    kf_ref[...] = k
    kb_ref[...] = k.astype(BF16)
    v = proj(OFF_V, KV_DIM)
    vf_ref[...] = v
    vb = v.astype(BF16)
    if v_transposed:
        r_e = lax.broadcasted_iota(jnp.int32, (N_KV_HEADS * VT_ROWS, LANES), 0)
        c_e = lax.broadcasted_iota(jnp.int32, (N_KV_HEADS * VT_ROWS, LANES), 1)
        head, d = r_e // VT_ROWS, r_e % VT_ROWS
        sel = jnp.where((d < HEAD_DIM) & (c_e == head * HEAD_DIM + d), 1.0, 0.0).astype(BF16)
        vt = lax.dot_general(sel, vb, (((1,), (1,)), ((), ())), preferred_element_type=F32)
        r_o = lax.broadcasted_iota(jnp.int32, (N_KV_HEADS * VT_ROWS, tm), 0)
        vb_ref[...] = jnp.where(r_o % VT_ROWS == HEAD_DIM, 1.0, vt).astype(BF16)
    else:
        vb_ref[...] = vb

    qi = _rope(proj(OFF_QI, IDX_HEADS * IDX_DIM), _tile_lanes(cos, 2), _tile_lanes(sn, 2),
               _tile_lanes(sp, 2)) * IDX_SCALE
    qi_hi = qi.astype(BF16).astype(F32)
    qi_lo = qi - qi_hi
    for t in range(IDX_HEADS // 2):
        h_t = qi_hi[:, t * LANES:(t + 1) * LANES]
        l_t = qi_lo[:, t * LANES:(t + 1) * LANES]
        a0 = jnp.where(left, h_t, pltpu.roll(l_t, IDX_DIM, 1))
        a1 = jnp.where(left, pltpu.roll(h_t, IDX_DIM, 1), l_t)
        for j, a in enumerate((a0, a1)):
            base = (2 * t + j) * 2 * LANES
            qia_ref[:, base:base + 2 * LANES] = jnp.concatenate([a, a], axis=1).astype(BF16)

    slab = proj(OFF_KIW, LANES)
    mu = jnp.sum(jnp.where(left, slab, 0.0), axis=1, keepdims=True) * (1.0 / IDX_DIM)
    d = jnp.where(left, slab - mu, 0.0)
    var = jnp.sum(d * d, axis=1, keepdims=True) * (1.0 / IDX_DIM)
    kin = d * lax.rsqrt(var + LN_EPS) * kig_ref[...] + kib_ref[...]
    slab = jnp.where(left, kin, slab)
    slab = _rope(slab, jnp.where(left, cos, 1.0), jnp.where(left, sn, 0.0), jnp.where(left, sp, 0.0))
    kiw_ref[...] = slab
    kz = jnp.where(left, slab, 0.0)
    k_hi = kz.astype(BF16).astype(F32)
    k_lo = kz - k_hi
    hh = k_hi + pltpu.roll(k_hi, IDX_DIM, 1)
    ll = k_lo + pltpu.roll(k_lo, IDX_DIM, 1)
    kihl_ref[...] = jnp.concatenate([hh, ll], axis=1).astype(BF16)

    u = proj(OFF_U, GMLP_DIM)
    vn = _layer_norm(proj(OFF_VG, GMLP_DIM), sg_ref[...], sb_ref[...])
    if vn_refs:
        vn_refs[0][...] = vn
    vnb = vn.astype(BF16)
    r_i = lax.broadcasted_iota(jnp.int32, (cl, cl), 0)
    c_i = lax.broadcasted_iota(jnp.int32, (cl, cl), 1)
    for g in range(GMLP_GROUPS):
        wg = jnp.where(r_i >= c_i, ws_ref[g], 0.0).astype(BF16)
        bg = bst_ref[:, g:g + 1]
        gs = slice(g * GMLP_GROUP_DIM, (g + 1) * GMLP_GROUP_DIM)
        for c in range(tm // cl):
            rs = slice(c * cl, (c + 1) * cl)
            mix = jnp.dot(wg, vnb[rs, gs], preferred_element_type=F32) + bg
            ob_ref[rs, gs] = (u[rs, gs] * mix).astype(BF16)


def _layer_spec(a, layer):
    return pl.BlockSpec((None,) + a.shape[1:], lambda i: (layer,) + (0,) * (a.ndim - 1),
                        pipeline_mode=pl.Buffered(1))


def _inproj(x2d, w_a, tabs, kig, kib, sg, sb, ws, bst, *, layer, tm, cl, emit_vn, v_transposed):
    m = x2d.shape[0]
    cos, sn, sp = tabs
    nt = cos.shape[0] // tm
    row = lambda w: pl.BlockSpec((tm, w), lambda i: (i, 0))
    tab = pl.BlockSpec((tm, LANES), lambda i: (i % nt, 0))
    full = lambda a: _layer_spec(a, layer)
    if v_transposed:
        vb_shape = jax.ShapeDtypeStruct((m // tm, N_KV_HEADS * VT_ROWS, tm), BF16)
        vb_spec = pl.BlockSpec((None, N_KV_HEADS * VT_ROWS, tm), lambda i: (i, 0, 0))
    else:
        vb_shape, vb_spec = jax.ShapeDtypeStruct((m, KV_DIM), BF16), row(KV_DIM)
    out_shapes = [
        jax.ShapeDtypeStruct((m, N_HEADS * LANES), BF16),
        jax.ShapeDtypeStruct((m, KV_DIM), F32),
        jax.ShapeDtypeStruct((m, KV_DIM), F32),
        jax.ShapeDtypeStruct((m, KV_DIM), BF16),
        vb_shape,
        jax.ShapeDtypeStruct((m, IDX_HEADS * 4 * IDX_DIM), BF16),
        jax.ShapeDtypeStruct((m, LANES), F32),
        jax.ShapeDtypeStruct((m, 4 * IDX_DIM), BF16),
        jax.ShapeDtypeStruct((m, GMLP_DIM), BF16),
    ]
    out_specs = [row(N_HEADS * LANES), row(KV_DIM), row(KV_DIM), row(KV_DIM), vb_spec,
                 row(IDX_HEADS * 4 * IDX_DIM), row(LANES), row(4 * IDX_DIM), row(GMLP_DIM)]
    if emit_vn:
        out_shapes.append(jax.ShapeDtypeStruct((m, GMLP_DIM), F32))
        out_specs.append(row(GMLP_DIM))
    return pl.pallas_call(
        functools.partial(_inproj_body, cl=cl, v_transposed=v_transposed),
        grid=(m // tm,),
        in_specs=[row(D_MODEL), full(w_a), tab, tab, tab, full(kig), full(kib), full(sg), full(sb),
                  full(ws), full(bst)],
        out_specs=out_specs,
        out_shape=out_shapes,
        compiler_params=_cparams(("parallel",)),
        name="inproj",
    )(x2d, w_a, cos, sn, sp, kig, kib, sg, sb, ws, bst)


def _nt_dot(a, b):
    return lax.dot_general(a, b, (((1,), (1,)), ((), ())), preferred_element_type=F32)


def _tree_sum(parts):
    while len(parts) > 1:
        parts = [parts[j] + parts[j + 1] for j in range(0, len(parts) - 1, 2)] + (
            [parts[-1]] if len(parts) % 2 else [])
    return parts[0]


def _attn_body(wit_ref, qp_ref, qia_ref, kihl_ref, kb_ref, vt_ref, o_ref,
               keys_ref, bias_ref, s_ref, *, qb, n_valid, pos0, s_true, topk):
    i = pl.program_id(1)
    sc_n = SEARCH_CHUNK
    cpt = KEY_TILE // sc_n
    grp = N_HEADS // N_KV_HEADS
    qlane = lax.broadcasted_iota(jnp.int32, (1, qb), 1)
    qpos = pos0 + i * qb + qlane
    limit = jnp.minimum((qpos // CHUNK + 1) * CHUNK, s_true)
    kv_len = jnp.minimum(((pos0 + i * qb + qb - 1) // CHUNK + 1) * CHUNK, s_true)
    n_kt = (kv_len + KEY_TILE - 1) // KEY_TILE
    n_c = (kv_len + sc_n - 1) // sc_n
    krow = lax.broadcasted_iota(jnp.int32, (sc_n, 1), 0)

    wit = wit_ref[...] * WI_SCALE
    hw = 4 * IDX_DIM
    pairs = [jnp.concatenate([qia_ref[:, (2 * hp) * hw:(2 * hp + 1) * hw],
                              qia_ref[:, (2 * hp + 1) * hw:(2 * hp + 2) * hw]], axis=0)
             for hp in range(IDX_HEADS // 2)]

    def score_tiles(tiles):
        for t in tiles:
            kt = kihl_ref[pl.ds(pl.multiple_of(t * KEY_TILE, KEY_TILE), KEY_TILE), :]
            sc = jnp.zeros((KEY_TILE, qb), F32)
            for hp, a2 in enumerate(pairs):
                dots = jnp.maximum(_nt_dot(kt, a2), 0.0)
                sc = (sc + wit[2 * hp:2 * hp + 1, :] * dots[:, :qb]
                      + wit[2 * hp + 1:2 * hp + 2, :] * dots[:, qb:])
            for j in range(cpt):
                kidx = t * KEY_TILE + j * sc_n + krow
                s_j = jnp.where(kidx < limit, sc[j * sc_n:(j + 1) * sc_n], NEG_INF)
                s_j = jnp.where(s_j == 0.0, 0.0, s_j)
                bits = lax.bitcast_convert_type(s_j, jnp.int32)
                keys_ref[t * cpt + j] = bits ^ ((bits >> 31) & 0x7FFFFFFF)

    n_pair = n_kt // 2
    odd = n_kt % 2 == 1

    def score_pair(pr, carry):
        score_tiles((2 * pr, 2 * pr + 1))
        return carry

    lax.fori_loop(0, n_pair, score_pair, 0)
    pl.when(odd)(lambda: score_tiles((n_kt - 1,)))

    n_acc = 4

    def count(pred, thr):
        def one_chunk(c, accs):
            ones = jnp.where(pred(keys_ref[c], thr), 1.0, 0.0)
            rows = [ones[SUBLANES * j:SUBLANES * (j + 1)] for j in range(sc_n // SUBLANES)]
            per = len(rows) // n_acc
            return tuple(a + _tree_sum(rows[k * per:(k + 1) * per]) for k, a in enumerate(accs))

        def many_chunks(g, accs):
            for j in range(SEARCH_UNROLL):
                accs = one_chunk(g * SEARCH_UNROLL + j, accs)
            return accs

        n_big = n_c // SEARCH_UNROLL
        accs = tuple(jnp.zeros((SUBLANES, qb), F32) for _ in range(n_acc))
        accs = lax.fori_loop(0, n_big, many_chunks, accs)
        accs = lax.fori_loop(n_big * SEARCH_UNROLL, n_c, one_chunk, accs)
        return jnp.sum(_tree_sum(list(accs)), axis=0, keepdims=True)

    def bit_step(it, carry):
        thr, best = carry
        cand = thr + jnp.left_shift(jnp.int32(1), 31 - it)
        cnt = count(lambda kk, t_: kk >= t_, cand)
        ok = cnt >= topk
        return jnp.where(ok, cand, thr), jnp.where(ok, cnt, best)

    thr, n_ge = lax.fori_loop(0, 32, bit_step,
                              (jnp.full((1, qb), -2 ** 31, jnp.int32), jnp.zeros((1, qb), F32)))

    real = qlane < n_valid
    overflow = jnp.max(jnp.where(real, n_ge, 0.0)) > topk

    @pl.when(jnp.logical_not(overflow))
    def _():
        def body(c, carry):
            sel = (keys_ref[c] >= thr) & (c * sc_n + krow < limit)
            bias_ref[c] = jnp.where(sel, 0.0, NEG_INF)
            return carry
        lax.fori_loop(0, n_kt * cpt, body, 0)

    @pl.when(overflow)
    def _():
        need = topk - count(lambda kk, t_: kk > t_, thr)
        r_i = lax.broadcasted_iota(jnp.int32, (sc_n, sc_n), 0)
        c_i = lax.broadcasted_iota(jnp.int32, (sc_n, sc_n), 1)
        lower = jnp.where(c_i <= r_i, 1.0, 0.0).astype(BF16)

        def tie_chunk(c, off):
            kk = keys_ref[c]
            eq = kk == thr
            local = jnp.dot(lower, jnp.where(eq, 1.0, 0.0).astype(BF16), preferred_element_type=F32)
            take = eq & (local + off <= need)
            sel = ((kk > thr) | take) & (c * sc_n + krow < limit)
            bias_ref[c] = jnp.where(sel, 0.0, NEG_INF)
            return off + local[sc_n - 1:sc_n, :]

        lax.fori_loop(0, n_kt * cpt, tie_chunk, jnp.zeros((1, qb), F32))

    r_e = lax.broadcasted_iota(jnp.int32, (LANES, LANES), 0)
    c_e = lax.broadcasted_iota(jnp.int32, (LANES, LANES), 1)
    eye = jnp.where(r_e == c_e, 1.0, 0.0).astype(BF16)
    kv_heads = range(N_KV_HEADS)
    for part in range(qb // LANES):
        q0 = part * LANES
        part_len = jnp.minimum(((pos0 + i * qb + q0 + LANES - 1) // CHUNK + 1) * CHUNK, s_true)
        part_kt = (part_len + KEY_TILE - 1) // KEY_TILE
        part_pair = part_kt // 2
        part_odd = part_kt % 2 == 1
        qns = [jnp.concatenate(
            [qp_ref[q0:q0 + LANES, (grp * n + g) * LANES:(grp * n + g + 1) * LANES] for g in range(grp)],
            axis=0) for n in kv_heads]

        def pass_a(tiles, ms, q0=q0, qns=qns):
            ms = list(ms)
            for t in tiles:
                kt = kb_ref[pl.ds(pl.multiple_of(t * KEY_TILE, KEY_TILE), KEY_TILE), :]
                bias = jnp.concatenate([bias_ref[t * cpt + j, :, q0:q0 + LANES] for j in range(cpt)], axis=0)
                bias = jnp.concatenate([bias] * grp, axis=1)
                for n in kv_heads:
                    s = _nt_dot(kt, qns[n]) + bias
                    s_ref[n, t] = s
                    ms[n] = jnp.maximum(ms[n], jnp.max(s, axis=0, keepdims=True))
            return tuple(ms)

        def pass_b(tiles, accs, ms):
            accs = list(accs)
            for t in tiles:
                for n in kv_heads:
                    p = jnp.exp2(s_ref[n, t] - ms[n]).astype(BF16)
                    vt = vt_ref[t, n * VT_ROWS:(n + 1) * VT_ROWS, :]
                    accs[n] = accs[n] + jnp.dot(vt, p, preferred_element_type=F32)
            return tuple(accs)

        ms = tuple(jnp.full((1, grp * LANES), M_INIT, F32) for _ in kv_heads)
        ms = lax.fori_loop(0, part_pair, lambda pr, c: pass_a((2 * pr, 2 * pr + 1), c), ms)
        ms = lax.cond(part_odd, lambda c: pass_a((part_kt - 1,), c), lambda c: c, ms)
        accs = tuple(jnp.zeros((VT_ROWS, grp * LANES), F32) for _ in kv_heads)
        accs = lax.fori_loop(0, part_pair, lambda pr, c, ms=ms: pass_b((2 * pr, 2 * pr + 1), c, ms), accs)
        accs = lax.cond(part_odd, lambda c, ms=ms: pass_b((part_kt - 1,), c, ms), lambda c: c, accs)

        for n in kv_heads:
            acc = accs[n]
            ot = (acc[:HEAD_DIM] / acc[HEAD_DIM:HEAD_DIM + 1]).astype(BF16)
            for g in range(grp):
                h = grp * n + g
                o_ref[q0:q0 + LANES, h * HEAD_DIM:(h + 1) * HEAD_DIM] = _nt_dot(
                    eye, ot[:, g * LANES:(g + 1) * LANES]).astype(o_ref.dtype)


def _attention(wit, qp, qia, kihl, kb, vt, *, qb, n_valid, pos0, s_true, topk):
    b, t, _ = qp.shape
    s_pad = kihl.shape[1]
    assert t % qb == 0 and s_pad % KEY_TILE == 0
    grp = N_HEADS // N_KV_HEADS
    qblk = lambda w: pl.BlockSpec((None, qb, w), lambda bi, i: (bi, i, 0))
    kblk = lambda w: pl.BlockSpec((None, s_pad, w), lambda bi, i: (bi, 0, 0))
    return pl.pallas_call(
        functools.partial(_attn_body, qb=qb, n_valid=n_valid, pos0=pos0, s_true=s_true, topk=topk),
        grid=(b, t // qb),
        in_specs=[pl.BlockSpec((None, None, SUBLANES, qb), lambda bi, i: (bi, i, 0, 0)),
                  qblk(N_HEADS * LANES), qblk(IDX_HEADS * 4 * IDX_DIM),
                  kblk(4 * IDX_DIM), kblk(KV_DIM),
                  pl.BlockSpec((None, s_pad // KEY_TILE, N_KV_HEADS * VT_ROWS, KEY_TILE),
                               lambda bi, i: (bi, 0, 0, 0))],
        out_specs=qblk(ATTN_DIM),
        out_shape=jax.ShapeDtypeStruct((b, t, ATTN_DIM), BF16),
        scratch_shapes=[
            pltpu.VMEM((s_pad // SEARCH_CHUNK, SEARCH_CHUNK, qb), jnp.int32),
            pltpu.VMEM((s_pad // SEARCH_CHUNK, SEARCH_CHUNK, qb), F32),
            pltpu.VMEM((N_KV_HEADS, s_pad // KEY_TILE, KEY_TILE, grp * LANES), F32),
        ],
        compiler_params=_cparams(("parallel", "arbitrary")),
        name="sparse_attn",
    )(wit, qp, qia, kihl, kb, vt)


FF_SPLIT = 4


def _post_body(x_ref, oa_ref, ob_ref, wga_ref, wgb_ref, wpa_ref, wpb_ref, wout_ref, g1_ref, b1_ref,
               w1_ref, w2_ref, g2_ref, b2_ref, o_ref):
    x = x_ref[...]
    xb = x.astype(BF16)
    ga = jnp.dot(xb, wga_ref[...], preferred_element_type=F32)
    gb = jnp.dot(xb, wgb_ref[...], preferred_element_type=F32)
    pa = jnp.dot(oa_ref[...], wpa_ref[...], preferred_element_type=F32)
    pb = jnp.dot(ob_ref[...], wpb_ref[...], preferred_element_type=F32)
    merged = jax.nn.sigmoid(ga) * pa + jax.nn.sigmoid(gb) * pb
    h = ALPHA * x + jnp.dot(merged.astype(BF16), wout_ref[...], preferred_element_type=F32)
    x1 = _layer_norm(h, g1_ref[...], b1_ref[...])
    x1b = x1.astype(BF16)
    cw = D_FF // FF_SPLIT
    ff = jnp.zeros(x.shape, F32)
    for c in range(FF_SPLIT):
        hcol = jnp.maximum(jnp.dot(x1b, w1_ref[:, c * cw:(c + 1) * cw], preferred_element_type=F32), 0.0)
        ff = ff + jnp.dot((hcol * hcol).astype(BF16), w2_ref[c * cw:(c + 1) * cw, :],
                          preferred_element_type=F32)
    o_ref[...] = _layer_norm(ALPHA * x1 + ff, g2_ref[...], b2_ref[...])


def _post_block(x2d, oa, ob, wga, wgb, wpa, wpb, wout, g1, b1, w1, w2, g2, b2, *, layer, tm):
    m = x2d.shape[0]
    row = lambda w: pl.BlockSpec((tm, w), lambda i: (i, 0))
    full = lambda a: _layer_spec(a, layer)
    consts = (wga, wgb, wpa, wpb, wout, g1, b1, w1, w2, g2, b2)
    return pl.pallas_call(
        _post_body,
        grid=(m // tm,),
        in_specs=[row(D_MODEL), row(ATTN_DIM), row(GMLP_DIM)] + [full(a) for a in consts],
        out_specs=row(D_MODEL),
        out_shape=jax.ShapeDtypeStruct((m, D_MODEL), F32),
        compiler_params=_cparams(("parallel",)),
        name="post_block",
    )(x2d, oa, ob, *consts)


def _rope_tables(pos, rows):
    half = ROPE_DIM // 2
    freqs = ROPE_THETA ** (-jnp.arange(half, dtype=F32) * 2.0 / ROPE_DIM)
    ang = pos.astype(F32)[:, None] * freqs[None, :]
    cos, sin = jnp.cos(ang), jnp.sin(ang)
    t = pos.shape[0]
    rest = HEAD_DIM - ROPE_DIM
    cos_h = jnp.concatenate([cos, cos, jnp.ones((t, rest), F32)], axis=1)
    sn_h = jnp.concatenate([-sin, jnp.zeros((t, half + rest), F32)], axis=1)
    sp_h = jnp.concatenate([jnp.zeros((t, half), F32), sin, jnp.zeros((t, rest), F32)], axis=1)
    reps = (max(rows // t, 1), LANES // HEAD_DIM)
    return tuple(jnp.tile(a, reps) for a in (cos_h, sn_h, sp_h))


def _split_w_in(w):
    wb = w.astype(BF16)
    n_head = OFF_KIW + IDX_DIM + IDX_HEADS
    col = lax.broadcasted_iota(jnp.int32, (1, 1, OFF_U), 2)
    head = jnp.where(col < n_head, wb[..., :OFF_U], jnp.zeros((), BF16))
    tail = wb[..., n_head:]
    w_a = jnp.concatenate([head, tail[..., :2 * GMLP_DIM]], axis=-1)
    return w_a, tail[..., 2 * GMLP_DIM:2 * GMLP_DIM + D_MODEL], tail[..., 2 * GMLP_DIM + D_MODEL:]


def _split_hi_lo(ki):
    hi = ki.astype(BF16)
    lo = (ki - hi.astype(F32)).astype(BF16)
    return jnp.concatenate([hi, hi, lo, lo], axis=-1)


def _head_weights_t(kiw, qb):
    b, t, _ = kiw.shape
    wi = kiw[:, :, IDX_DIM:IDX_DIM + IDX_HEADS].reshape(b, t // qb, qb, IDX_HEADS)
    return jnp.pad(wi.transpose(0, 1, 3, 2), ((0, 0), (0, 0), (0, SUBLANES - IDX_HEADS), (0, 0)))


def _augment_vt(vt):
    b, nt, _, kt = vt.shape
    v4 = vt.reshape(b, nt, N_KV_HEADS, HEAD_DIM, kt)
    ones = jnp.ones((b, nt, N_KV_HEADS, 1, kt), vt.dtype)
    zeros = jnp.zeros((b, nt, N_KV_HEADS, VT_ROWS - HEAD_DIM - 1, kt), vt.dtype)
    return jnp.concatenate([v4, ones, zeros], axis=3).reshape(b, nt, N_KV_HEADS * VT_ROWS, kt)


def _pad_keys(a, s_pad):
    return jnp.pad(a, ((0, 0), (0, s_pad - a.shape[1]), (0, 0)))


def kernel(x_prompt, x_sample, cache_k, cache_v, cache_idx_k, w_in, idx_k_g, idx_k_b, sgu_ln_g, sgu_ln_b,
           w_s, b_s, w_pa, w_pb, w_out, ln1_g, ln1_b, w_ff1, w_ff2, ln2_g, ln2_b):
    bp, tp, _ = x_prompt.shape
    bs, ts, _ = x_sample.shape
    depth = w_in.shape[0]
    past = cache_k.shape[2]
    topk_p = min(TOPK_MAX, tp // 4)
    topk_s = min(TOPK_MAX, (past + ts) // 4)
    tm_p = KEY_TILE
    tm_s = bs * ts
    assert (bp * tp) % tm_p == 0 and tp % KEY_TILE == 0
    s_all = past + ts
    s_pad = -(-s_all // KEY_TILE) * KEY_TILE
    qb_p = Q_BLOCK_PROMPT if tp % Q_BLOCK_PROMPT == 0 else Q_BLOCK

    tabs_p = _rope_tables(jnp.arange(tp), tm_p)
    tabs_s = _rope_tables(past + jnp.arange(ts), tm_s)

    xp = x_prompt.reshape(bp * tp, D_MODEL)
    xs = x_sample.reshape(bs * ts, D_MODEL)
    pk, pv, pik, sk, sv, sik, ssv = [], [], [], [], [], [], []
    row3 = lambda a: a.reshape(depth, 1, -1)
    w_a, w_ga, w_gb = _split_w_in(w_in)
    kig = row3(jnp.pad(idx_k_g, ((0, 0), (0, LANES - IDX_DIM))))
    kib = row3(jnp.pad(idx_k_b, ((0, 0), (0, LANES - IDX_DIM))))
    sg, sb = row3(sgu_ln_g), row3(sgu_ln_b)
    wpa, wpb, wout = w_pa.astype(BF16), w_pb.astype(BF16), w_out.astype(BF16)
    w1, w2 = w_ff1.astype(BF16), w_ff2.astype(BF16)
    g1, b1, g2, b2 = row3(ln1_g), row3(ln1_b), row3(ln2_g), row3(ln2_b)
    bst_p = b_s.transpose(0, 2, 1)
    ws_s, bst_s = w_s[:, :, :ts, :ts], b_s[:, :, :ts].transpose(0, 2, 1)
    for l in range(depth):
        qp, kf, vf, kb, vt, qia, kiw, kihl, ob = _inproj(
            xp, w_a, tabs_p, kig, kib, sg, sb, w_s, bst_p, layer=l, tm=tm_p, cl=GMLP_CHUNK, emit_vn=False,
            v_transposed=True)
        r3 = lambda a: a.reshape(bp, tp, a.shape[-1])
        oa = _attention(_head_weights_t(r3(kiw), qb_p), r3(qp), r3(qia), r3(kihl), r3(kb),
                        vt.reshape(bp, tp // KEY_TILE, N_KV_HEADS * VT_ROWS, KEY_TILE),
                        qb=qb_p, n_valid=qb_p, pos0=0, s_true=tp, topk=topk_p)
        xp = _post_block(xp, oa.reshape(bp * tp, ATTN_DIM), ob, w_ga, w_gb, wpa, wpb, wout, g1, b1,
                         w1, w2, g2, b2, layer=l, tm=tm_p)
        pk.append(kf.reshape(bp, tp, N_KV_HEADS, HEAD_DIM))
        pv.append(vf.reshape(bp, tp, N_KV_HEADS, HEAD_DIM))
        pik.append(kiw[:, :IDX_DIM].reshape(bp, tp, IDX_DIM))

        qp, kf, vf, kb, vb, qia, kiw, kihl, ob, vn = _inproj(
            xs, w_a, tabs_s, kig, kib, sg, sb, ws_s, bst_s, layer=l,
            tm=tm_s, cl=ts, emit_vn=True, v_transposed=False)
        r3 = lambda a: a.reshape(bs, ts, a.shape[-1])
        k_all = _pad_keys(jnp.concatenate([cache_k[l].reshape(bs, past, KV_DIM).astype(BF16), r3(kb)], 1), s_pad)
        v_all = _pad_keys(jnp.concatenate([cache_v[l].reshape(bs, past, KV_DIM).astype(BF16), r3(vb)], 1), s_pad)
        vt_all = _augment_vt(v_all.reshape(bs, s_pad // KEY_TILE, KEY_TILE, KV_DIM).transpose(0, 1, 3, 2))
        ki_all = _pad_keys(jnp.concatenate([_split_hi_lo(cache_idx_k[l]), r3(kihl)], 1), s_pad)
        padq = lambda a: jnp.pad(r3(a), ((0, 0), (0, Q_BLOCK - ts), (0, 0)))
        oa = _attention(_head_weights_t(padq(kiw), Q_BLOCK), padq(qp), padq(qia), ki_all, k_all, vt_all,
                        qb=Q_BLOCK, n_valid=ts, pos0=past, s_true=s_all, topk=topk_s)[:, :ts]
        xs = _post_block(xs, oa.reshape(bs * ts, ATTN_DIM), ob, w_ga, w_gb, wpa, wpb, wout, g1, b1,
                         w1, w2, g2, b2, layer=l, tm=tm_s)
        sk.append(kf.reshape(bs, ts, N_KV_HEADS, HEAD_DIM))
        sv.append(vf.reshape(bs, ts, N_KV_HEADS, HEAD_DIM))
        sik.append(kiw[:, :IDX_DIM].reshape(bs, ts, IDX_DIM))
        ssv.append(vn.reshape(bs, ts, GMLP_DIM))

    return (xp.reshape(bp, tp, D_MODEL), xs.reshape(bs, ts, D_MODEL),
            jnp.stack(pk), jnp.stack(pv), jnp.stack(pik),
            jnp.stack(sk), jnp.stack(sv), jnp.stack(sik), jnp.stack(ssv))
```
